```python
import math
import jax, jax.numpy as jnp
from jax import lax
import numpy as np

D_MODEL = 1024
BATCH = 32
SEQ = 2048
DEPTH = 2

CHUNK = 64
Q_BLOCK = 128

FOX_HEADS = 4
FOX_HEAD_DIM = 64
SB_HEADS = 4
SB_HEAD_DIM = 64
GDN_HEADS = 4
GDN_KEY_DIM = 128
GDN_VAL_DIM = 128
CONV_WIDTH = 4

FOX_W = FOX_HEADS * FOX_HEAD_DIM
SB_W = SB_HEADS * SB_HEAD_DIM
GDN_KW = GDN_HEADS * GDN_KEY_DIM
GDN_VW = GDN_HEADS * GDN_VAL_DIM
MIX_W = FOX_W + SB_W + GDN_VW
N_BRANCH = 3
FFN_HIDDEN = 2816
N_SUBLAYERS = 3

IN_SIZES = (FOX_W, FOX_W, FOX_W, FOX_HEADS,
            SB_W, SB_W, SB_W,
            2 * GDN_KW + GDN_VW, GDN_HEADS, GDN_HEADS,
            GDN_VW,
            N_BRANCH * D_MODEL)
IN_COLS = sum(IN_SIZES)

DEEPNORM_ALPHA = (2.0 * DEPTH) ** 0.25
DEEPNORM_BETA = (8.0 * DEPTH) ** -0.25
LN_EPS = 1e-5
RMS_EPS = 1e-6

kernel_name = 'hybrid_fox_sb_gdn_macaron_deepnorm'


def _layer_norm(x, g, b):
    xf = x.astype(jnp.float32)
    mu = jnp.mean(xf, -1, keepdims=True)
    var = jnp.mean(jnp.square(xf - mu), -1, keepdims=True)
    return ((xf - mu) * lax.rsqrt(var + LN_EPS) * g.astype(jnp.float32) + b.astype(jnp.float32)).astype(x.dtype)


def _l2norm(t):
    return t * lax.rsqrt(jnp.sum(jnp.square(t), -1, keepdims=True) + RMS_EPS)


def _heads(t, h):
    B, S, _ = t.shape
    return t.reshape(B, S, h, -1).transpose(0, 2, 1, 3)


def _q_blocks(t):
    B, H, S = t.shape[:3]
    t = t.reshape((B, H, S // Q_BLOCK, Q_BLOCK) + t.shape[3:])
    return jnp.moveaxis(t, 2, 0)


def _merge_blocks(o):
    nb, B, H, qb, d = o.shape
    return o.transpose(1, 0, 3, 2, 4).reshape(B, nb * qb, H * d)


def _forgetting_attention(q, k, v, log_f):
    S, d = q.shape[2], q.shape[3]
    scale = d ** -0.5
    cum = jnp.cumsum(log_f, axis=-1)
    key_pos = jnp.arange(S)

    def block(args):
        qb, cq, i = args
        qpos = i * Q_BLOCK + jnp.arange(Q_BLOCK)
        s = jnp.einsum('bhqd,bhkd->bhqk', qb, k).astype(jnp.float32) * scale
        s = s + cq[..., :, None] - cum[..., None, :]
        s = jnp.where(key_pos[None, :] <= qpos[:, None], s, -jnp.inf)
        p = jax.nn.softmax(s, axis=-1)
        return jnp.einsum('bhqk,bhkd->bhqd', p.astype(v.dtype), v)

    nb = S // Q_BLOCK
    o = lax.map(block, (_q_blocks(q), _q_blocks(cum), jnp.arange(nb)))
    return _merge_blocks(o)


def _stick_breaking_attention(q, k, v):
    S, d = q.shape[2], q.shape[3]
    scale = d ** -0.5
    key_pos = jnp.arange(S)

    def block(args):
        qb, i = args
        qpos = i * Q_BLOCK + jnp.arange(Q_BLOCK)
        z = jnp.einsum('bhqd,bhkd->bhqk', qb, k).astype(jnp.float32) * scale
        mask = key_pos[None, :] < qpos[:, None]
        log_beta = jax.nn.log_sigmoid(z)
        log_rem = jnp.where(mask, jax.nn.log_sigmoid(-z), 0.0)
        after = lax.cumsum(log_rem, axis=log_rem.ndim - 1, reverse=True) - log_rem
        w = jnp.where(mask, jnp.exp(log_beta + after), 0.0)
        return jnp.einsum('bhqk,bhkd->bhqd', w.astype(v.dtype), v)

    nb = S // Q_BLOCK
    o = lax.map(block, (_q_blocks(q), jnp.arange(nb)))
    return _merge_blocks(o)


def _gated_delta_rule(q, k, v, g, beta):
    f32 = jnp.float32
    B, H, S, dk = q.shape
    dv = v.shape[-1]
    n = S // CHUNK
    q = q.reshape(B, H, n, CHUNK, dk)
    k = k.reshape(B, H, n, CHUNK, dk)
    v = v.reshape(B, H, n, CHUNK, dv)
    beta = beta.reshape(B, H, n, CHUNK)
    gc = jnp.cumsum(g.reshape(B, H, n, CHUNK), axis=-1)
    tri_incl = jnp.tril(jnp.ones((CHUNK, CHUNK), bool))
    tri_strict = jnp.tril(jnp.ones((CHUNK, CHUNK), bool), -1)
    diff = gc[..., :, None] - gc[..., None, :]
    decay = jnp.where(tri_incl, jnp.exp(jnp.where(tri_incl, diff, 0.0)), 0.0)
    k_beta = k * beta[..., None]
    lower = jnp.where(tri_strict, jnp.einsum('bhnid,bhnjd->bhnij', k_beta, k) * decay, 0.0) + jnp.eye(CHUNK, dtype=f32)
    rhs = jnp.concatenate([v * beta[..., None], k_beta * jnp.exp(gc)[..., None]], axis=-1)
    sol = lax.linalg.triangular_solve(lower, rhs, left_side=True, lower=True, unit_diagonal=True)
    u, w = sol[..., :dv], sol[..., dv:]
    intra = jnp.where(tri_incl, jnp.einsum('bhnid,bhnjd->bhnij', q, k) * decay, 0.0)
    q_dec = q * jnp.exp(gc)[..., None]
    k_dec = k * jnp.exp(gc[..., -1:] - gc)[..., None]
    chunk_decay = jnp.exp(gc[..., -1])

    def step(state, xs):
        u_c, w_c, intra_c, qd_c, kd_c, dec_c = xs
        v_new = u_c - jnp.einsum('bhck,bhkv->bhcv', w_c, state)
        o_c = jnp.einsum('bhck,bhkv->bhcv', qd_c, state) + jnp.einsum('bhij,bhjv->bhiv', intra_c, v_new)
        state = state * dec_c[..., None, None] + jnp.einsum('bhck,bhcv->bhkv', kd_c, v_new)
        return state, o_c

    xs = tuple(jnp.moveaxis(t, 2, 0) for t in (u, w, intra, q_dec, k_dec, chunk_decay))
    _, o = lax.scan(step, jnp.zeros((B, H, dk, dv), f32), xs)
    return o.transpose(1, 0, 3, 2, 4).reshape(B, S, H, dv)


def _causal_conv(x, w):
    ch = x.shape[-1]
    return lax.conv_general_dilated(x, w[:, None, :].astype(x.dtype), window_strides=(1,),
                                    padding=[(CONV_WIDTH - 1, 0)],
                                    dimension_numbers=('NWC', 'WIO', 'NWC'),
                                    feature_group_count=ch)


def _gdn_branch(qkv, beta_logit, a_logit, gate, conv_w, a_log, dt_bias, norm_g):
    f32 = jnp.float32
    B, S, _ = qkv.shape
    out_dtype = qkv.dtype
    qkv = jax.nn.silu(_causal_conv(qkv, conv_w)).astype(f32)
    q, k, v = jnp.split(qkv, [GDN_KW, 2 * GDN_KW], axis=-1)
    q = _l2norm(_heads(q, GDN_HEADS)) * GDN_KEY_DIM ** -0.5
    k = _l2norm(_heads(k, GDN_HEADS))
    v = _heads(v, GDN_HEADS)
    beta = jax.nn.sigmoid(beta_logit.astype(f32)).transpose(0, 2, 1)
    g = (-jnp.exp(a_log.astype(f32)) * jax.nn.softplus(a_logit.astype(f32) + dt_bias.astype(f32))).transpose(0, 2, 1)
    o = _gated_delta_rule(q, k, v, g, beta)
    o = o * lax.rsqrt(jnp.mean(jnp.square(o), -1, keepdims=True) + RMS_EPS) * norm_g.astype(f32)
    o = o.reshape(B, S, GDN_VW) * jax.nn.silu(gate.astype(f32))
    return o.astype(out_dtype)


def _mixer(h, w_in, b_forget, conv_w, gdn_a_log, gdn_dt_bias, gdn_norm_g, w_branch, w_o):
    f32 = jnp.float32
    B, S, _ = h.shape
    split_idx = [int(i) for i in np.cumsum(IN_SIZES)[:-1]]
    (fq, fk, fv, ff, sq, sk, sv, gqkv, gbeta, ga, ggate, bgates) = jnp.split(h @ w_in, split_idx, axis=-1)
    log_f = jax.nn.log_sigmoid(ff.astype(f32) + b_forget.astype(f32)).transpose(0, 2, 1)
    o_a = _forgetting_attention(_heads(fq, FOX_HEADS), _heads(fk, FOX_HEADS), _heads(fv, FOX_HEADS), log_f)
    o_b = _stick_breaking_attention(_heads(sq, SB_HEADS), _heads(sk, SB_HEADS), _heads(sv, SB_HEADS))
    o_c = _gdn_branch(gqkv, gbeta, ga, ggate, conv_w, gdn_a_log, gdn_dt_bias, gdn_norm_g)
    y_a = o_a @ w_branch[:FOX_W]
    y_b = o_b @ w_branch[FOX_W:FOX_W + SB_W]
    y_c = o_c @ w_branch[FOX_W + SB_W:]
    gates = jax.nn.sigmoid(bgates).reshape(B, S, N_BRANCH, D_MODEL)
    merged = gates[:, :, 0] * y_a + gates[:, :, 1] * y_b + gates[:, :, 2] * y_c
    return merged @ w_o


def _swiglu(h, w_up, w_down):
    gate, up = jnp.split(h @ w_up, 2, axis=-1)
    return (jax.nn.silu(gate) * up) @ w_down


def _modulate(x, m):
    return x * (1.0 + m[:, 1]) + m[:, 0]


def setup_inputs(seed: int = 0) -> dict:
    key = jax.random.key(seed)
    ks = jax.random.split(key, 20)
    f32 = jnp.float32

    def nrm(k, shape, s):
        return jax.random.normal(k, shape, f32) * s

    x = nrm(ks[0], (BATCH, SEQ, D_MODEL), 1.0)
    c = nrm(ks[1], (BATCH, D_MODEL), 1.0)
    w_ada = nrm(ks[2], (DEPTH, D_MODEL, N_SUBLAYERS * 3 * D_MODEL), 0.2 * D_MODEL ** -0.5)
    b_ada = nrm(ks[3], (DEPTH, N_SUBLAYERS * 3 * D_MODEL), 0.02)
    ln_g = 1.0 + nrm(ks[4], (DEPTH, N_SUBLAYERS, D_MODEL), 0.02)
    ln_b = nrm(ks[5], (DEPTH, N_SUBLAYERS, D_MODEL), 0.02)
    ffn_w_up = nrm(ks[6], (DEPTH, 2, D_MODEL, 2 * FFN_HIDDEN), D_MODEL ** -0.5)
    ffn_w_down = nrm(ks[7], (DEPTH, 2, FFN_HIDDEN, D_MODEL), DEEPNORM_BETA * FFN_HIDDEN ** -0.5)
    w_in = nrm(ks[8], (DEPTH, D_MODEL, IN_COLS), D_MODEL ** -0.5)
    b_forget = jax.random.uniform(ks[9], (DEPTH, FOX_HEADS), f32, 1.0, 5.0)
    conv_w = nrm(ks[10], (DEPTH, CONV_WIDTH, 2 * GDN_KW + GDN_VW), CONV_WIDTH ** -0.5)
    gdn_a_log = jnp.log(jax.random.uniform(ks[11], (DEPTH, GDN_HEADS), f32, 1.0, 16.0))
    dt = jnp.exp(jax.random.uniform(ks[12], (DEPTH, GDN_HEADS), f32, math.log(1e-3), math.log(1e-1)))
    gdn_dt_bias = dt + jnp.log(-jnp.expm1(-dt))
    gdn_norm_g = 1.0 + nrm(ks[13], (DEPTH, GDN_VAL_DIM), 0.02)
    w_branch = jnp.concatenate([nrm(ks[14], (DEPTH, FOX_W, D_MODEL), FOX_W ** -0.5),
                                nrm(ks[15], (DEPTH, SB_W, D_MODEL), SB_W ** -0.5),
                                nrm(ks[16], (DEPTH, GDN_VW, D_MODEL), GDN_VW ** -0.5)], axis=1)
    w_o = nrm(ks[17], (DEPTH, D_MODEL, D_MODEL), DEEPNORM_BETA * D_MODEL ** -0.5)
    return {'x': x, 'c': c, 'w_ada': w_ada, 'b_ada': b_ada, 'ln_g': ln_g, 'ln_b': ln_b,
            'ffn_w_up': ffn_w_up, 'ffn_w_down': ffn_w_down, 'w_in': w_in, 'b_forget': b_forget,
            'conv_w': conv_w, 'gdn_a_log': gdn_a_log, 'gdn_dt_bias': gdn_dt_bias,
            'gdn_norm_g': gdn_norm_g, 'w_branch': w_branch, 'w_o': w_o}


def reference(x, c, w_ada, b_ada, ln_g, ln_b, ffn_w_up, ffn_w_down, w_in, b_forget, conv_w,
              gdn_a_log, gdn_dt_bias, gdn_norm_g, w_branch, w_o):
    B = x.shape[0]
    sc = jax.nn.silu(c)
    for l in range(DEPTH):
        mod = (sc @ w_ada[l] + b_ada[l]).reshape(B, N_SUBLAYERS, 3, D_MODEL)[:, :, :, None, :]
        m = mod[:, 0]
        y = _swiglu(_modulate(x, m), ffn_w_up[l, 0], ffn_w_down[l, 0])
        x = _layer_norm(DEEPNORM_ALPHA * x + 0.5 * (1.0 + m[:, 2]) * y, ln_g[l, 0], ln_b[l, 0])
        m = mod[:, 1]
        y = _mixer(_modulate(x, m), w_in[l], b_forget[l], conv_w[l], gdn_a_log[l], gdn_dt_bias[l],
                   gdn_norm_g[l], w_branch[l], w_o[l])
        x = _layer_norm(DEEPNORM_ALPHA * x + (1.0 + m[:, 2]) * y, ln_g[l, 1], ln_b[l, 1])
        m = mod[:, 2]
        y = _swiglu(_modulate(x, m), ffn_w_up[l, 1], ffn_w_down[l, 1])
        x = _layer_norm(DEEPNORM_ALPHA * x + 0.5 * (1.0 + m[:, 2]) * y, ln_g[l, 2], ln_b[l, 2])
    return x
```

```python
import functools

import jax
import jax.numpy as jnp
from jax import lax
from jax.experimental import pallas as pl
from jax.experimental.pallas import tpu as pltpu

f32 = jnp.float32
bf16 = jnp.bfloat16

D_MODEL = 1024
DEPTH = 2
N_SUB = 3
FFN_HIDDEN = 2816
HEADS = 4
ATT_W = 256
ATT_HEAD = 64
GDN_W = 512
GDN_HEAD = 128
CHUNK = 64
CONV_WIDTH = 4
SMALL_W = 128
DEEPNORM_ALPHA = (2.0 * DEPTH) ** 0.25
LN_EPS = 1e-5
RMS_EPS = 1e-6

FFN_TM = 512
FFN_HC = 256
PROJ_TM = 512
MERGE_TM = 512
ATT_TQ = 256
ATT_TK = 256
PREFIX_BLK = 256
GDN_SB = 128
GDN_TS = 1024
CONV_HALO = 8
NEG_BIG = -1e30
VMEM_LIMIT = 56 * 1024 * 1024


def _dot(a, b):
    return jnp.dot(a, b, preferred_element_type=f32)


def _dot_nt(a, b):
    return lax.dot_general(a, b, (((1,), (1,)), ((), ())), preferred_element_type=f32)


def _dot_tn(a, b):
    return lax.dot_general(a, b, (((0,), (0,)), ((), ())), preferred_element_type=f32)


def _split2(x):
    hi = x.astype(bf16)
    lo = (x - hi.astype(f32)).astype(bf16)
    return hi, lo


def _split3(x):
    hi = x.astype(bf16)
    r = x - hi.astype(f32)
    mid = r.astype(bf16)
    lo = (r - mid.astype(f32)).astype(bf16)
    return hi, mid, lo


def _dot3(a, b):
    ah, al = _split2(a)
    bh, bl = _split2(b)
    return _dot(ah, bh) + (_dot(ah, bl) + _dot(al, bh))


def _layer_norm(r, g, b):
    mu = jnp.mean(r, axis=-1, keepdims=True)
    d = r - mu
    var = jnp.mean(d * d, axis=-1, keepdims=True)
    return d * lax.rsqrt(var + LN_EPS) * g + b


def _softplus_neg_abs(x):
    return jnp.log1p(jnp.exp(-jnp.abs(x)))


def _const_spec(shape):
    nd = len(shape)
    return pl.BlockSpec(shape, lambda *_: (0,) * nd, pipeline_mode=pl.Buffered(1))


def _params(sem):
    return pltpu.CompilerParams(dimension_semantics=sem, vmem_limit_bytes=VMEM_LIMIT)


def _ada_kernel(c_ref, w_ref, b_ref, o_ref):
    c = c_ref[...]
    sc = c * jax.nn.sigmoid(c)
    o_ref[...] = _dot3(sc, w_ref[...]) + b_ref[...]


def _ada(c, w_ada, b_ada):
    depth, d, n = w_ada.shape
    bsz = c.shape[0]
    tn = 1024
    return pl.pallas_call(
        _ada_kernel,
        grid=(depth, n // tn),
        in_specs=[pl.BlockSpec((bsz, d), lambda l, j: (0, 0)),
                  pl.BlockSpec((None, d, tn), lambda l, j: (l, 0, j)),
                  pl.BlockSpec((None, 1, tn), lambda l, j: (l, 0, j))],
        out_specs=pl.BlockSpec((None, bsz, tn), lambda l, j: (l, 0, j)),
        out_shape=jax.ShapeDtypeStruct((depth, bsz, n), f32),
        compiler_params=_params(("parallel", "parallel")),
        name="adaln",
    )(c, w_ada, b_ada.reshape(depth, 1, n))


def _ffn_kernel(x_ref, mod_ref, wg_ref, wu_ref, wd_ref, lng_ref, lnb_ref, o_ref, h_ref, acc_ref):
    x = x_ref[...]
    h_ref[...] = (x * (1.0 + mod_ref[1:2, :]) + mod_ref[0:1, :]).astype(bf16)
    n_chunks = wg_ref.shape[0]
    for j in range(n_chunks):
        h = h_ref[...]
        g = _dot(h, wg_ref[j])
        u = _dot(h, wu_ref[j])
        a = (g * jax.nn.sigmoid(g) * u).astype(bf16)
        y = _dot(a, wd_ref[j])
        if j == 0:
            acc_ref[...] = y
        else:
            acc_ref[...] += y
    r = DEEPNORM_ALPHA * x + (0.5 * (1.0 + mod_ref[2:3, :])) * acc_ref[...]
    o_ref[...] = _layer_norm(r, lng_ref[...], lnb_ref[...])


def _ffn(x, mod, wg, wu, wd, lng, lnb):
    bsz, s, d = x.shape
    tm = FFN_TM
    return pl.pallas_call(
        _ffn_kernel,
        grid=(bsz, s // tm),
        in_specs=[pl.BlockSpec((None, tm, d), lambda b, i: (b, i, 0)),
                  pl.BlockSpec((None, 3, d), lambda b, i: (b, 0, 0)),
                  _const_spec(wg.shape), _const_spec(wu.shape), _const_spec(wd.shape),
                  _const_spec((1, d)), _const_spec((1, d))],
        out_specs=pl.BlockSpec((None, tm, d), lambda b, i: (b, i, 0)),
        out_shape=jax.ShapeDtypeStruct(x.shape, f32),
        scratch_shapes=[pltpu.VMEM((tm, d), bf16), pltpu.VMEM((tm, d), f32)],
        compiler_params=_params(("parallel", "parallel")),
        name="ffn",
    )(x, mod, wg, wu, wd, lng, lnb)


def _proj_kernel(x_ref, mod_ref, wa_ref, wb_ref, wc_ref, ws_ref, cw_ref,
                 fq_ref, fkt_ref, fv_ref, sq_ref, skt_ref, sv_ref,
                 gq_ref, gk_ref, gv_ref, sm_ref, h_ref, cbuf_ref):
    tm = x_ref.shape[0]
    h_ref[...] = (x_ref[...] * (1.0 + mod_ref[1:2, :]) + mod_ref[0:1, :]).astype(bf16)

    def attn_group(w_ref, q_ref, kt_ref, v_ref):
        r = _dot(h_ref[...], w_ref[...])
        q_ref[...] = (r[:, :ATT_W] * (ATT_HEAD ** -0.5)).astype(bf16)
        kt_ref[...] = r[:, ATT_W:2 * ATT_W].T.astype(bf16)
        v_ref[...] = r[:, 2 * ATT_W:].astype(bf16)

    attn_group(wa_ref, fq_ref, fkt_ref, fv_ref)
    attn_group(wb_ref, sq_ref, skt_ref, sv_ref)
    sm_ref[...] = _dot(h_ref[...], ws_ref[...])

    @pl.when(pl.program_id(1) == 0)
    def _():
        cbuf_ref[0:CONV_HALO, :] = jnp.zeros((CONV_HALO, cbuf_ref.shape[1]), f32)

    cbuf_ref[CONV_HALO:, :] = _dot(h_ref[...], wc_ref[...])
    conv = cbuf_ref[CONV_HALO:, :] * cw_ref[CONV_WIDTH - 1:CONV_WIDTH, :]
    for tap in range(CONV_WIDTH - 1):
        back = CONV_WIDTH - 1 - tap
        conv = conv + cbuf_ref[CONV_HALO - back:CONV_HALO - back + tm, :] * cw_ref[tap:tap + 1, :]
    cbuf_ref[0:CONV_HALO, :] = cbuf_ref[tm:tm + CONV_HALO, :]
    act = conv * jax.nn.sigmoid(conv)
    for hd in range(HEADS):
        lo = hd * GDN_HEAD
        q = act[:, lo:lo + GDN_HEAD]
        k = act[:, GDN_W + lo:GDN_W + lo + GDN_HEAD]
        gq_ref[:, lo:lo + GDN_HEAD] = (q * lax.rsqrt(jnp.sum(q * q, -1, keepdims=True) + RMS_EPS)
                                       * (GDN_HEAD ** -0.5))
        gk_ref[:, lo:lo + GDN_HEAD] = k * lax.rsqrt(jnp.sum(k * k, -1, keepdims=True) + RMS_EPS)
    gv_ref[...] = act[:, 2 * GDN_W:]


def _proj(x, mod, wa, wb, wc, ws, cw):
    bsz, s, d = x.shape
    tm = PROJ_TM
    row = lambda w: pl.BlockSpec((None, tm, w), lambda b, i: (b, i, 0))
    col = pl.BlockSpec((None, ATT_W, tm), lambda b, i: (b, 0, i))
    sd = jax.ShapeDtypeStruct
    return pl.pallas_call(
        _proj_kernel,
        grid=(bsz, s // tm),
        in_specs=[row(d), pl.BlockSpec((None, 3, d), lambda b, i: (b, 0, 0)),
                  _const_spec(wa.shape), _const_spec(wb.shape), _const_spec(wc.shape),
                  _const_spec(ws.shape), _const_spec(cw.shape)],
        out_specs=[row(ATT_W), col, row(ATT_W), row(ATT_W), col, row(ATT_W),
                   row(GDN_W), row(GDN_W), row(GDN_W), row(SMALL_W)],
        out_shape=[sd((bsz, s, ATT_W), bf16), sd((bsz, ATT_W, s), bf16), sd((bsz, s, ATT_W), bf16),
                   sd((bsz, s, ATT_W), bf16), sd((bsz, ATT_W, s), bf16), sd((bsz, s, ATT_W), bf16),
                   sd((bsz, s, GDN_W), f32), sd((bsz, s, GDN_W), f32), sd((bsz, s, GDN_W), f32),
                   sd((bsz, s, SMALL_W), f32)],
        scratch_shapes=[pltpu.VMEM((tm, d), bf16), pltpu.VMEM((tm + CONV_HALO, 3 * GDN_W), f32)],
        compiler_params=_params(("parallel", "arbitrary")),
        name="mixer_in_proj",
    )(x, mod, wa, wb, wc, ws, cw)


def _prefix_kernel(sm_ref, pv_ref, p_ref, pt_ref):
    s = sm_ref.shape[0]
    blk = PREFIX_BLK
    ri = lax.broadcasted_iota(jnp.int32, (blk, blk), 0)
    ci = lax.broadcasted_iota(jnp.int32, (blk, blk), 1)
    low = ri >= ci
    same = (ri // CHUNK) == (ci // CHUNK)
    tri_full = jnp.where(low, 1.0, 0.0).astype(bf16)
    tri_chunk = jnp.where(low & same, 1.0, 0.0).astype(bf16)
    ones_chunk = jnp.where(same, 1.0, 0.0).astype(bf16)
    lane = lax.broadcasted_iota(jnp.int32, (blk, SMALL_W), 1)
    b_forget = pv_ref[0:1, :]
    neg_a = -jnp.exp(pv_ref[1:2, :])
    dt_bias = pv_ref[2:3, :]
    carry = jnp.zeros((1, SMALL_W), f32)
    for i in range(s // blk):
        x = sm_ref[i * blk:(i + 1) * blk, :]
        xf = x + b_forget
        log_f = jnp.minimum(xf, 0.0) - _softplus_neg_abs(xf)
        beta = jax.nn.sigmoid(x)
        xg = x + dt_bias
        g = neg_a * (jnp.maximum(xg, 0.0) + _softplus_neg_abs(xg))
        val = jnp.where(lane < 4, log_f, jnp.where(lane < 8, beta, jnp.where(lane < 16, g, 0.0)))
        parts = _split3(val)
        cum_full = sum(_dot(tri_full, p) for p in parts) + carry
        cum_chunk = sum(_dot(tri_chunk, p) for p in parts)
        tot_chunk = sum(_dot(ones_chunk, p) for p in parts)
        carry = cum_full[blk - 1:blk, :]
        out = jnp.where(lane < 4, cum_full,
                        jnp.where(lane < 8, val, jnp.where(lane < 12, cum_chunk, tot_chunk)))
        p_ref[i * blk:(i + 1) * blk, :] = out
        pt_ref[:, i * blk:(i + 1) * blk] = out.T[0:16, :]


def _prefix(sm, pv):
    bsz, s, _ = sm.shape
    return pl.pallas_call(
        _prefix_kernel,
        grid=(bsz,),
        in_specs=[pl.BlockSpec((None, s, SMALL_W), lambda b: (b, 0, 0)), _const_spec(pv.shape)],
        out_specs=[pl.BlockSpec((None, s, SMALL_W), lambda b: (b, 0, 0)),
                   pl.BlockSpec((None, 16, s), lambda b: (b, 0, 0))],
        out_shape=[jax.ShapeDtypeStruct((bsz, s, SMALL_W), f32),
                   jax.ShapeDtypeStruct((bsz, 16, s), f32)],
        compiler_params=_params(("parallel",)),
        name="gate_prefix",
    )(sm, pv)


def _head_masks(shape, axis):
    lane = lax.broadcasted_iota(jnp.int32, shape, axis)
    return [(lane >= hd * ATT_HEAD) & (lane < (hd + 1) * ATT_HEAD) for hd in range(HEADS)]


def _by_head(masks, cols):
    out = cols[HEADS - 1]
    for hd in range(HEADS - 2, -1, -1):
        out = jnp.where(masks[hd], cols[hd], out)
    return out


def _fox_kernel(q_ref, kt_ref, v_ref, p_ref, pt_ref, o_ref):
    tq, tk = ATT_TQ, ATT_TK
    i = pl.program_id(1)
    q = q_ref[...]
    qmask = _head_masks((tq, ATT_W), 1)
    vmask = _head_masks((tk, ATT_W), 1)
    zero = jnp.zeros((), bf16)
    qm = [jnp.where(qmask[hd], q, zero) for hd in range(HEADS)]
    cq = [p_ref[:, hd:hd + 1] for hd in range(HEADS)]
    rows = lax.broadcasted_iota(jnp.int32, (tq, tk), 0)
    cols = lax.broadcasted_iota(jnp.int32, (tq, tk), 1)
    causal = cols <= rows

    def block(j, carry, diag):
        m, l, acc = carry
        off = pl.multiple_of(j * tk, tk)
        kt = kt_ref[:, pl.ds(off, tk)]
        v = v_ref[pl.ds(off, tk), :]
        m_new, l_new, alpha, pv = [], [], [], None
        for hd in range(HEADS):
            s = _dot(qm[hd], kt) + cq[hd] - pt_ref[hd:hd + 1, pl.ds(off, tk)]
            if diag:
                s = jnp.where(causal, s, NEG_BIG)
            mh = jnp.maximum(m[hd], jnp.max(s, axis=-1, keepdims=True))
            p = jnp.exp(s - mh)
            a = jnp.exp(m[hd] - mh)
            m_new.append(mh)
            alpha.append(a)
            l_new.append(a * l[hd] + jnp.sum(p, axis=-1, keepdims=True))
            c = _dot(p.astype(bf16), jnp.where(vmask[hd], v, zero))
            pv = c if pv is None else pv + c
        acc = acc * _by_head(qmask, alpha) + pv
        return tuple(m_new), tuple(l_new), acc

    init = (tuple(jnp.full((tq, 1), NEG_BIG, f32) for _ in range(HEADS)),
            tuple(jnp.zeros((tq, 1), f32) for _ in range(HEADS)),
            jnp.zeros((tq, ATT_W), f32))
    carry = lax.fori_loop(0, i, functools.partial(block, diag=False), init)
    _, l, acc = block(i, carry, diag=True)
    o_ref[...] = (acc / _by_head(qmask, l)).astype(bf16)


def _fox(q, kt, v, p, pt):
    bsz, s, _ = q.shape
    tq = ATT_TQ
    return pl.pallas_call(
        _fox_kernel,
        grid=(bsz, s // tq),
        in_specs=[pl.BlockSpec((None, tq, ATT_W), lambda b, i: (b, i, 0)),
                  pl.BlockSpec((None, ATT_W, s), lambda b, i: (b, 0, 0)),
                  pl.BlockSpec((None, s, ATT_W), lambda b, i: (b, 0, 0)),
                  pl.BlockSpec((None, tq, SMALL_W), lambda b, i: (b, i, 0)),
                  pl.BlockSpec((None, 16, s), lambda b, i: (b, 0, 0))],
        out_specs=pl.BlockSpec((None, tq, ATT_W), lambda b, i: (b, i, 0)),
        out_shape=jax.ShapeDtypeStruct((bsz, s, ATT_W), bf16),
        compiler_params=_params(("parallel", "parallel")),
        name="forgetting_attention",
    )(q, kt, v, p, pt)


def _sb_kernel(q_ref, kt_ref, v_ref, o_ref):
    tq, tk = ATT_TQ, ATT_TK
    i = pl.program_id(1)
    q = q_ref[...]
    qmask = _head_masks((tq, ATT_W), 1)
    vmask = _head_masks((tk, ATT_W), 1)
    zero = jnp.zeros((), bf16)
    qm = [jnp.where(qmask[hd], q, zero) for hd in range(HEADS)]
    rows = lax.broadcasted_iota(jnp.int32, (tq, tk), 0)
    cols = lax.broadcasted_iota(jnp.int32, (tq, tk), 1)
    strict = cols < rows
    kr = lax.broadcasted_iota(jnp.int32, (tk, tk), 0)
    kc = lax.broadcasted_iota(jnp.int32, (tk, tk), 1)
    later = jnp.where(kr > kc, 1.0, 0.0).astype(bf16)

    def block(j, carry, diag):
        rem, acc = carry
        off = pl.multiple_of(j * tk, tk)
        kt = kt_ref[:, pl.ds(off, tk)]
        v = v_ref[pl.ds(off, tk), :]
        rem_new = []
        for hd in range(HEADS):
            z = _dot(qm[hd], kt)
            log_beta = jnp.minimum(z, 0.0) - _softplus_neg_abs(z)
            log_rem = log_beta - z
            if diag:
                log_rem = jnp.where(strict, log_rem, 0.0)
            hi, lo = _split2(log_rem)
            after = _dot(hi, later) + _dot(lo, later) + rem[hd]
            w = jnp.exp(log_beta + after)
            if diag:
                w = jnp.where(strict, w, 0.0)
            rem_new.append(rem[hd] + jnp.sum(log_rem, axis=-1, keepdims=True))
            acc = acc + _dot(w.astype(bf16), jnp.where(vmask[hd], v, zero))
        return tuple(rem_new), acc

    init = (tuple(jnp.zeros((tq, 1), f32) for _ in range(HEADS)), jnp.zeros((tq, ATT_W), f32))
    carry = block(i, init, diag=True)
    _, acc = lax.fori_loop(0, i, lambda jj, c: block(i - 1 - jj, c, diag=False), carry)
    o_ref[...] = acc.astype(bf16)


def _sb(q, kt, v):
    bsz, s, _ = q.shape
    tq = ATT_TQ
    return pl.pallas_call(
        _sb_kernel,
        grid=(bsz, s // tq),
        in_specs=[pl.BlockSpec((None, tq, ATT_W), lambda b, i: (b, i, 0)),
                  pl.BlockSpec((None, ATT_W, s), lambda b, i: (b, 0, 0)),
                  pl.BlockSpec((None, s, ATT_W), lambda b, i: (b, 0, 0))],
        out_specs=pl.BlockSpec((None, tq, ATT_W), lambda b, i: (b, i, 0)),
        out_shape=jax.ShapeDtypeStruct((bsz, s, ATT_W), bf16),
        compiler_params=_params(("parallel", "parallel")),
        name="stick_breaking_attention",
    )(q, kt, v)


def _gdn_kernel(q_ref, k_ref, v_ref, p_ref, pt_ref, ng_ref, o_ref,
                u_ref, w_ref, qd_ref, kd_ref, intra_ref, dec_ref, state_ref):
    ts = q_ref.shape[0]
    sb = GDN_SB
    n_sb = ts // sb
    ri = lax.broadcasted_iota(jnp.int32, (sb, sb), 0)
    ci = lax.broadcasted_iota(jnp.int32, (sb, sb), 1)
    same = (ri // CHUNK) == (ci // CHUNK)
    incl = same & (ri >= ci)
    strict = same & (ri > ci)
    eye = jnp.where(ri == ci, 1.0, 0.0).astype(f32)

    @pl.when(pl.program_id(1) == 0)
    def _():
        state_ref[...] = jnp.zeros(state_ref.shape, f32)

    def local(i, _):
        r0 = pl.multiple_of(i * sb, sb)
        for hd in range(HEADS):
            lanes = slice(hd * GDN_HEAD, (hd + 1) * GDN_HEAD)
            q = q_ref[pl.ds(r0, sb), lanes]
            k = k_ref[pl.ds(r0, sb), lanes]
            v = v_ref[pl.ds(r0, sb), lanes]
            beta = p_ref[pl.ds(r0, sb), 4 + hd:5 + hd]
            gc = p_ref[pl.ds(r0, sb), 8 + hd:9 + hd]
            gl = p_ref[pl.ds(r0, sb), 12 + hd:13 + hd]
            gc_row = pt_ref[8 + hd:9 + hd, pl.ds(r0, sb)]
            decay = jnp.where(incl, jnp.exp(jnp.where(incl, gc - gc_row, 0.0)), 0.0)
            kb = k * beta
            kbf = k.astype(bf16)
            a = jnp.where(strict, _dot_nt(kb.astype(bf16), kbf) * decay, 0.0)
            t = eye - a
            pw = a
            for _ in range(5):
                pw = _dot3(pw, pw)
                t = t + _dot3(t, pw)
            e_gc = jnp.exp(gc)
            rhs = jnp.concatenate([v * beta, kb * e_gc], axis=1)
            sol = _dot3(t, rhs)
            u_ref[pl.ds(r0, sb), lanes] = sol[:, :GDN_HEAD]
            w_ref[pl.ds(r0, sb), lanes] = sol[:, GDN_HEAD:].astype(bf16)
            intra = jnp.where(incl, _dot_nt(q.astype(bf16), kbf) * decay, 0.0)
            intra_ref[hd, pl.ds(r0, sb), :] = intra.astype(bf16)
            qd_ref[pl.ds(r0, sb), lanes] = (q * e_gc).astype(bf16)
            kd_ref[pl.ds(r0, sb), lanes] = (k * jnp.exp(gl - gc)).astype(bf16)
            glb = jnp.broadcast_to(gl, (sb, GDN_HEAD))
            dec_ref[hd, pl.ds(2 * i, 1), :] = jnp.exp(glb[0:1, :])
            dec_ref[hd, pl.ds(2 * i + 1, 1), :] = jnp.exp(glb[CHUNK:CHUNK + 1, :])
        return 0

    lax.fori_loop(0, n_sb, local, 0)

    def scan(i, _):
        r0 = pl.multiple_of(i * sb, sb)
        for hd in range(HEADS):
            lanes = slice(hd * GDN_HEAD, (hd + 1) * GDN_HEAD)
            v_first = None
            for c in range(2):
                rc = pl.multiple_of(r0 + c * CHUNK, CHUNK)
                state = state_ref[hd]
                sbf = state.astype(bf16)
                v_new = u_ref[pl.ds(rc, CHUNK), lanes] - _dot(w_ref[pl.ds(rc, CHUNK), lanes], sbf)
                vb = v_new.astype(bf16)
                if c == 0:
                    v_first = vb
                    inner = _dot(intra_ref[hd, pl.ds(rc, CHUNK), 0:CHUNK], vb)
                else:
                    inner = _dot(intra_ref[hd, pl.ds(rc, CHUNK), :],
                                 jnp.concatenate([v_first, vb], axis=0))
                o = _dot(qd_ref[pl.ds(rc, CHUNK), lanes], sbf) + inner
                state_ref[hd] = (state * dec_ref[hd, pl.ds(2 * i + c, 1), :]
                                 + _dot_tn(kd_ref[pl.ds(rc, CHUNK), lanes], vb))
                o = o * lax.rsqrt(jnp.mean(o * o, axis=-1, keepdims=True) + RMS_EPS) * ng_ref[...]
                o_ref[pl.ds(rc, CHUNK), lanes] = o
        return 0

    lax.fori_loop(0, n_sb, scan, 0)


def _gdn(q, k, v, p, pt, ng):
    bsz, s, _ = q.shape
    ts = min(GDN_TS, s)
    row = lambda w: pl.BlockSpec((None, ts, w), lambda b, i: (b, i, 0))
    return pl.pallas_call(
        _gdn_kernel,
        grid=(bsz, s // ts),
        in_specs=[row(GDN_W), row(GDN_W), row(GDN_W), row(SMALL_W),
                  pl.BlockSpec((None, 16, ts), lambda b, i: (b, 0, i)), _const_spec(ng.shape)],
        out_specs=row(GDN_W),
        out_shape=jax.ShapeDtypeStruct((bsz, s, GDN_W), f32),
        scratch_shapes=[pltpu.VMEM((ts, GDN_W), f32), pltpu.VMEM((ts, GDN_W), bf16),
                        pltpu.VMEM((ts, GDN_W), bf16), pltpu.VMEM((ts, GDN_W), bf16),
                        pltpu.VMEM((HEADS, ts, GDN_SB), bf16),
                        pltpu.VMEM((HEADS, ts // CHUNK, GDN_HEAD), f32),
                        pltpu.VMEM((HEADS, GDN_HEAD, GDN_HEAD), f32)],
        compiler_params=_params(("parallel", "arbitrary")),
        name="gated_delta_rule",
    )(q, k, v, p, pt, ng)


def _merge_kernel(x_ref, mod_ref, oa_ref, ob_ref, on_ref, wgate_ref, wgg_ref,
                  wba_ref, wbb_ref, wbc_ref, wo_ref, lng_ref, lnb_ref, o_ref, h_ref):
    x = x_ref[...]
    h_ref[...] = (x * (1.0 + mod_ref[1:2, :]) + mod_ref[0:1, :]).astype(bf16)
    gg = _dot(h_ref[...], wgg_ref[...])
    oc = (on_ref[...] * (gg * jax.nn.sigmoid(gg))).astype(bf16)
    branches = ((oa_ref[...], wba_ref), (ob_ref[...], wbb_ref), (oc, wbc_ref))
    merged = None
    for n, (o, wb_ref) in enumerate(branches):
        gate = jax.nn.sigmoid(_dot(h_ref[...], wgate_ref[n]))
        term = gate * _dot(o, wb_ref[...])
        merged = term if merged is None else merged + term
    y = _dot(merged.astype(bf16), wo_ref[...])
    r = DEEPNORM_ALPHA * x + (1.0 + mod_ref[2:3, :]) * y
    o_ref[...] = _layer_norm(r, lng_ref[...], lnb_ref[...])


def _merge(x, mod, oa, ob, on, wgate, wgg, wba, wbb, wbc, wo, lng, lnb):
    bsz, s, d = x.shape
    tm = MERGE_TM
    row = lambda w: pl.BlockSpec((None, tm, w), lambda b, i: (b, i, 0))
    return pl.pallas_call(
        _merge_kernel,
        grid=(bsz, s // tm),
        in_specs=[row(d), pl.BlockSpec((None, 3, d), lambda b, i: (b, 0, 0)),
                  row(ATT_W), row(ATT_W), row(GDN_W),
                  _const_spec(wgate.shape), _const_spec(wgg.shape), _const_spec(wba.shape),
                  _const_spec(wbb.shape), _const_spec(wbc.shape), _const_spec(wo.shape),
                  _const_spec((1, d)), _const_spec((1, d))],
        out_specs=row(d),
        out_shape=jax.ShapeDtypeStruct(x.shape, f32),
        scratch_shapes=[pltpu.VMEM((tm, d), bf16)],
        compiler_params=_params(("parallel", "parallel")),
        name="mixer_merge",
    )(x, mod, oa, ob, on, wgate, wgg, wba, wbb, wbc, wo, lng, lnb)


def _ffn_weights(w_up, w_down):
    d = w_up.shape[0]
    n_chunks = FFN_HIDDEN // FFN_HC
    chunked = lambda w: w.reshape(d, n_chunks, FFN_HC).transpose(1, 0, 2).astype(bf16)
    return (chunked(w_up[:, :FFN_HIDDEN]), chunked(w_up[:, FFN_HIDDEN:]),
            w_down.reshape(n_chunks, FFN_HC, d).astype(bf16))


def _mixer_weights(w_in, b_forget, a_log, dt_bias):
    d = w_in.shape[0]
    sizes = (ATT_W, ATT_W, ATT_W, HEADS, ATT_W, ATT_W, ATT_W, 3 * GDN_W, HEADS, HEADS, GDN_W,
             3 * D_MODEL)
    offs = [0]
    for n in sizes:
        offs.append(offs[-1] + n)
    piece = lambda n: w_in[:, offs[n]:offs[n + 1]]
    wa = jnp.concatenate([piece(0), piece(1), piece(2)], axis=1).astype(bf16)
    wb = jnp.concatenate([piece(4), piece(5), piece(6)], axis=1).astype(bf16)
    wc = piece(7).astype(bf16)
    ws = jnp.concatenate([piece(3), piece(8), piece(9), piece(9),
                          jnp.zeros((d, SMALL_W - 4 * HEADS), w_in.dtype)], axis=1).astype(bf16)
    wgg = piece(10).astype(bf16)
    wgate = piece(11).reshape(d, 3, D_MODEL).transpose(1, 0, 2).astype(bf16)
    pad = lambda vec, at: jnp.zeros((SMALL_W,), f32).at[at:at + HEADS].set(vec.astype(f32))
    pv = jnp.stack([pad(b_forget, 0),
                    pad(a_log, 2 * HEADS) + pad(a_log, 3 * HEADS),
                    pad(dt_bias, 2 * HEADS) + pad(dt_bias, 3 * HEADS)]
                   + [jnp.zeros((SMALL_W,), f32)] * 5)
    return wa, wb, wc, ws, wgg, wgate, pv


def kernel(x, c, w_ada, b_ada, ln_g, ln_b, ffn_w_up, ffn_w_down, w_in, b_forget, conv_w,
           gdn_a_log, gdn_dt_bias, gdn_norm_g, w_branch, w_o):
    bsz = x.shape[0]
    mod = _ada(c, w_ada, b_ada).reshape(DEPTH, bsz, N_SUB, 3, D_MODEL)
    for l in range(DEPTH):
        lng = lambda n: ln_g[l, n].reshape(1, D_MODEL)
        lnb = lambda n: ln_b[l, n].reshape(1, D_MODEL)
        x = _ffn(x, mod[l, :, 0], *_ffn_weights(ffn_w_up[l, 0], ffn_w_down[l, 0]), lng(0), lnb(0))

        wa, wb, wc, ws, wgg, wgate, pv = _mixer_weights(w_in[l], b_forget[l], gdn_a_log[l],
                                                        gdn_dt_bias[l])
        m = mod[l, :, 1]
        fq, fkt, fv, sq, skt, sv, gq, gk, gv, sm = _proj(x, m, wa, wb, wc, ws, conv_w[l])
        p, pt = _prefix(sm, pv)
        oa = _fox(fq, fkt, fv, p, pt)
        ob = _sb(sq, skt, sv)
        on = _gdn(gq, gk, gv, p, pt, gdn_norm_g[l].reshape(1, GDN_HEAD))
        wbr = w_branch[l].astype(bf16)
        x = _merge(x, m, oa, ob, on, wgate, wgg, wbr[:ATT_W], wbr[ATT_W:2 * ATT_W],
                   wbr[2 * ATT_W:], w_o[l].astype(bf16), lng(1), lnb(1))

        x = _ffn(x, mod[l, :, 2], *_ffn_weights(ffn_w_up[l, 1], ffn_w_down[l, 1]), lng(2), lnb(2))
    return x
```

```python
import functools

import jax
import jax.numpy as jnp
from jax import lax
from jax.experimental import pallas as pl
from jax.experimental.pallas import tpu as pltpu

f32 = jnp.float32
bf16 = jnp.bfloat16

D_MODEL = 1024
DEPTH = 2
N_SUB = 3
FFN_HIDDEN = 2816
HEADS = 4
ATT_W = 256
ATT_HEAD = 64
GDN_W = 512
GDN_HEAD = 128
CHUNK = 64
CONV_WIDTH = 4
SMALL_W = 128
DEEPNORM_ALPHA = (2.0 * DEPTH) ** 0.25
LN_EPS = 1e-5
RMS_EPS = 1e-6

FFN_TM = 512
FFN_HC = 256
PROJ_TM = 512
MERGE_TM = 512
ATT_TQ = 256
ATT_TK = 256
PREFIX_BLK = 256
GDN_SB = 128
GDN_TS = 1024
CONV_HALO = 8
NEG_BIG = -1e30
EXP_UNDERFLOW = -104.0
VMEM_LIMIT = 56 * 1024 * 1024


def _dot(a, b):
    return jnp.dot(a, b, preferred_element_type=f32)


def _dot_nt(a, b):
    return lax.dot_general(a, b, (((1,), (1,)), ((), ())), preferred_element_type=f32)


def _dot_tn(a, b):
    return lax.dot_general(a, b, (((0,), (0,)), ((), ())), preferred_element_type=f32)


def _split2(x):
    hi = x.astype(bf16)
    lo = (x - hi.astype(f32)).astype(bf16)
    return hi, lo


def _split3(x):
    hi = x.astype(bf16)
    r = x - hi.astype(f32)
    mid = r.astype(bf16)
    lo = (r - mid.astype(f32)).astype(bf16)
    return hi, mid, lo


def _dot3(a, b):
    ah, al = _split2(a)
    bh, bl = _split2(b)
    return _dot(ah, bh) + (_dot(ah, bl) + _dot(al, bh))


def _layer_norm(r, g, b):
    mu = jnp.mean(r, axis=-1, keepdims=True)
    d = r - mu
    var = jnp.mean(d * d, axis=-1, keepdims=True)
    return d * lax.rsqrt(var + LN_EPS) * g + b


def _softplus_neg_abs(x):
    return jnp.log1p(jnp.exp(-jnp.abs(x)))


def _const_spec(shape):
    nd = len(shape)
    return pl.BlockSpec(shape, lambda *_: (0,) * nd, pipeline_mode=pl.Buffered(1))


def _params(sem):
    return pltpu.CompilerParams(dimension_semantics=sem, vmem_limit_bytes=VMEM_LIMIT)


def _ada_kernel(c_ref, w_ref, b_ref, o_ref):
    c = c_ref[...]
    sc = c * jax.nn.sigmoid(c)
    o_ref[...] = _dot3(sc, w_ref[...]) + b_ref[...]


def _ada(c, w_ada, b_ada):
    depth, d, n = w_ada.shape
    bsz = c.shape[0]
    tn = 1024
    return pl.pallas_call(
        _ada_kernel,
        grid=(depth, n // tn),
        in_specs=[pl.BlockSpec((bsz, d), lambda l, j: (0, 0)),
                  pl.BlockSpec((None, d, tn), lambda l, j: (l, 0, j)),
                  pl.BlockSpec((None, 1, tn), lambda l, j: (l, 0, j))],
        out_specs=pl.BlockSpec((None, bsz, tn), lambda l, j: (l, 0, j)),
        out_shape=jax.ShapeDtypeStruct((depth, bsz, n), f32),
        compiler_params=_params(("parallel", "parallel")),
        name="adaln",
    )(c, w_ada, b_ada.reshape(depth, 1, n))


def _ffn_kernel(x_ref, mod_ref, wup_ref, wd_ref, lng_ref, lnb_ref, o_ref, h_ref, acc_ref):
    x = x_ref[...]
    h_ref[...] = (x * (1.0 + mod_ref[1:2, :]) + mod_ref[0:1, :]).astype(bf16)
    for j in range(FFN_HIDDEN // FFN_HC):
        lo = j * FFN_HC
        h = h_ref[...]
        g = _dot(h, wup_ref[:, lo:lo + FFN_HC])
        u = _dot(h, wup_ref[:, FFN_HIDDEN + lo:FFN_HIDDEN + lo + FFN_HC])
        a = (g * jax.nn.sigmoid(g) * u).astype(bf16)
        y = _dot(a, wd_ref[lo:lo + FFN_HC, :])
        if j == 0:
            acc_ref[...] = y
        else:
            acc_ref[...] += y
    r = DEEPNORM_ALPHA * x + (0.5 * (1.0 + mod_ref[2:3, :])) * acc_ref[...]
    o_ref[...] = _layer_norm(r, lng_ref[...], lnb_ref[...])


def _ffn(x, mod, wup, wd, lng, lnb):
    bsz, s, d = x.shape
    tm = FFN_TM
    return pl.pallas_call(
        _ffn_kernel,
        grid=(bsz, s // tm),
        in_specs=[pl.BlockSpec((None, tm, d), lambda b, i: (b, i, 0)),
                  pl.BlockSpec((None, 3, d), lambda b, i: (b, 0, 0)),
                  _const_spec(wup.shape), _const_spec(wd.shape),
                  _const_spec((1, d)), _const_spec((1, d))],
        out_specs=pl.BlockSpec((None, tm, d), lambda b, i: (b, i, 0)),
        out_shape=jax.ShapeDtypeStruct(x.shape, f32),
        scratch_shapes=[pltpu.VMEM((tm, d), bf16), pltpu.VMEM((tm, d), f32)],
        compiler_params=_params(("parallel", "parallel")),
        name="ffn",
    )(x, mod, wup, wd, lng, lnb)


def _proj_kernel(x_ref, mod_ref, wa_ref, wb_ref, wc_ref, ws_ref, cw_ref,
                 fq_ref, fkt_ref, fv_ref, sq_ref, skt_ref, sv_ref,
                 gq_ref, gk_ref, gv_ref, sm_ref, h_ref, cbuf_ref):
    tm = x_ref.shape[0]
    h_ref[...] = (x_ref[...] * (1.0 + mod_ref[1:2, :]) + mod_ref[0:1, :]).astype(bf16)

    def attn_group(w_ref, q_ref, kt_ref, v_ref):
        r = _dot(h_ref[...], w_ref[...])
        q_ref[...] = (r[:, :ATT_W] * (ATT_HEAD ** -0.5)).astype(bf16)
        kt_ref[...] = r[:, ATT_W:2 * ATT_W].T.astype(bf16)
        v_ref[...] = r[:, 2 * ATT_W:].astype(bf16)

    attn_group(wa_ref, fq_ref, fkt_ref, fv_ref)
    attn_group(wb_ref, sq_ref, skt_ref, sv_ref)
    sm_ref[...] = _dot(h_ref[...], ws_ref[...])

    @pl.when(pl.program_id(1) == 0)
    def _():
        cbuf_ref[0:CONV_HALO, :] = jnp.zeros((CONV_HALO, cbuf_ref.shape[1]), f32)

    cbuf_ref[CONV_HALO:, :] = _dot(h_ref[...], wc_ref[...])
    conv = cbuf_ref[CONV_HALO:, :] * cw_ref[CONV_WIDTH - 1:CONV_WIDTH, :]
    for tap in range(CONV_WIDTH - 1):
        back = CONV_WIDTH - 1 - tap
        conv = conv + cbuf_ref[CONV_HALO - back:CONV_HALO - back + tm, :] * cw_ref[tap:tap + 1, :]
    cbuf_ref[0:CONV_HALO, :] = cbuf_ref[tm:tm + CONV_HALO, :]
    act = conv * jax.nn.sigmoid(conv)
    for hd in range(HEADS):
        lo = hd * GDN_HEAD
        q = act[:, lo:lo + GDN_HEAD]
        k = act[:, GDN_W + lo:GDN_W + lo + GDN_HEAD]
        gq_ref[:, lo:lo + GDN_HEAD] = (q * lax.rsqrt(jnp.sum(q * q, -1, keepdims=True) + RMS_EPS)
                                       * (GDN_HEAD ** -0.5))
        gk_ref[:, lo:lo + GDN_HEAD] = k * lax.rsqrt(jnp.sum(k * k, -1, keepdims=True) + RMS_EPS)
    gv_ref[...] = act[:, 2 * GDN_W:]


def _proj(x, mod, wa, wb, wc, ws, cw):
    bsz, s, d = x.shape
    tm = PROJ_TM
    row = lambda w: pl.BlockSpec((None, tm, w), lambda b, i: (b, i, 0))
    col = pl.BlockSpec((None, ATT_W, tm), lambda b, i: (b, 0, i))
    sd = jax.ShapeDtypeStruct
    return pl.pallas_call(
        _proj_kernel,
        grid=(bsz, s // tm),
        in_specs=[row(d), pl.BlockSpec((None, 3, d), lambda b, i: (b, 0, 0)),
                  _const_spec(wa.shape), _const_spec(wb.shape), _const_spec(wc.shape),
                  _const_spec(ws.shape), _const_spec(cw.shape)],
        out_specs=[row(ATT_W), col, row(ATT_W), row(ATT_W), col, row(ATT_W),
                   row(GDN_W), row(GDN_W), row(GDN_W), row(SMALL_W)],
        out_shape=[sd((bsz, s, ATT_W), bf16), sd((bsz, ATT_W, s), bf16), sd((bsz, s, ATT_W), bf16),
                   sd((bsz, s, ATT_W), bf16), sd((bsz, ATT_W, s), bf16), sd((bsz, s, ATT_W), bf16),
                   sd((bsz, s, GDN_W), f32), sd((bsz, s, GDN_W), f32), sd((bsz, s, GDN_W), f32),
                   sd((bsz, s, SMALL_W), f32)],
        scratch_shapes=[pltpu.VMEM((tm, d), bf16), pltpu.VMEM((tm + CONV_HALO, 3 * GDN_W), f32)],
        compiler_params=_params(("parallel", "arbitrary")),
        name="mixer_in_proj",
    )(x, mod, wa, wb, wc, ws, cw)


def _prefix_kernel(sm_ref, pv_ref, p_ref, pt_ref):
    s = sm_ref.shape[0]
    blk = PREFIX_BLK
    ri = lax.broadcasted_iota(jnp.int32, (blk, blk), 0)
    ci = lax.broadcasted_iota(jnp.int32, (blk, blk), 1)
    low = ri >= ci
    same = (ri // CHUNK) == (ci // CHUNK)
    tri_full = jnp.where(low, 1.0, 0.0).astype(bf16)
    tri_chunk = jnp.where(low & same, 1.0, 0.0).astype(bf16)
    ones_chunk = jnp.where(same, 1.0, 0.0).astype(bf16)
    lane = lax.broadcasted_iota(jnp.int32, (blk, SMALL_W), 1)
    b_forget = pv_ref[0:1, :]
    neg_a = -jnp.exp(pv_ref[1:2, :])
    dt_bias = pv_ref[2:3, :]
    carry = jnp.zeros((1, SMALL_W), f32)
    for i in range(s // blk):
        x = sm_ref[i * blk:(i + 1) * blk, :]
        xf = x + b_forget
        log_f = jnp.minimum(xf, 0.0) - _softplus_neg_abs(xf)
        beta = jax.nn.sigmoid(x)
        xg = x + dt_bias
        g = neg_a * (jnp.maximum(xg, 0.0) + _softplus_neg_abs(xg))
        val = jnp.where(lane < 4, log_f, jnp.where(lane < 8, beta, jnp.where(lane < 16, g, 0.0)))
        parts = _split3(val)
        cum_full = sum(_dot(tri_full, p) for p in parts) + carry
        cum_chunk = sum(_dot(tri_chunk, p) for p in parts)
        tot_chunk = sum(_dot(ones_chunk, p) for p in parts)
        carry = cum_full[blk - 1:blk, :]
        out = jnp.where(lane < 4, cum_full,
                        jnp.where(lane < 8, val, jnp.where(lane < 12, cum_chunk, tot_chunk)))
        p_ref[i * blk:(i + 1) * blk, :] = out
        pt_ref[:, i * blk:(i + 1) * blk] = out.T[0:16, :]


def _prefix(sm, pv):
    bsz, s, _ = sm.shape
    return pl.pallas_call(
        _prefix_kernel,
        grid=(bsz,),
        in_specs=[pl.BlockSpec((None, s, SMALL_W), lambda b: (b, 0, 0)), _const_spec(pv.shape)],
        out_specs=[pl.BlockSpec((None, s, SMALL_W), lambda b: (b, 0, 0)),
                   pl.BlockSpec((None, 16, s), lambda b: (b, 0, 0))],
        out_shape=[jax.ShapeDtypeStruct((bsz, s, SMALL_W), f32),
                   jax.ShapeDtypeStruct((bsz, 16, s), f32)],
        compiler_params=_params(("parallel",)),
        name="gate_prefix",
    )(sm, pv)


def _head_masks(shape, axis):
    lane = lax.broadcasted_iota(jnp.int32, shape, axis)
    return [(lane >= hd * ATT_HEAD) & (lane < (hd + 1) * ATT_HEAD) for hd in range(HEADS)]


def _by_head(masks, cols):
    out = cols[HEADS - 1]
    for hd in range(HEADS - 2, -1, -1):
        out = jnp.where(masks[hd], cols[hd], out)
    return out


def _fox_kernel(q_ref, kt_ref, v_ref, p_ref, pt_ref, o_ref):
    tq, tk = ATT_TQ, ATT_TK
    i = pl.program_id(1)
    q = q_ref[...]
    qmask = _head_masks((tq, ATT_W), 1)
    vmask = _head_masks((tk, ATT_W), 1)
    zero = jnp.zeros((), bf16)
    qm = [jnp.where(qmask[hd], q, zero) for hd in range(HEADS)]
    cq = [p_ref[:, hd:hd + 1] for hd in range(HEADS)]
    rows = lax.broadcasted_iota(jnp.int32, (tq, tk), 0)
    cols = lax.broadcasted_iota(jnp.int32, (tq, tk), 1)
    causal = cols <= rows

    def block(j, carry, diag):
        m, l, acc = carry
        off = pl.multiple_of(j * tk, tk)
        kt = kt_ref[:, pl.ds(off, tk)]
        v = v_ref[pl.ds(off, tk), :]
        m_new, l_new, alpha, pv = [], [], [], None
        for hd in range(HEADS):
            s = _dot(qm[hd], kt) + cq[hd] - pt_ref[hd:hd + 1, pl.ds(off, tk)]
            if diag:
                s = jnp.where(causal, s, NEG_BIG)
            mh = jnp.maximum(m[hd], jnp.max(s, axis=-1, keepdims=True))
            p = jnp.exp(s - mh)
            a = jnp.exp(m[hd] - mh)
            m_new.append(mh)
            alpha.append(a)
            l_new.append(a * l[hd] + jnp.sum(p, axis=-1, keepdims=True))
            c = _dot(p.astype(bf16), jnp.where(vmask[hd], v, zero))
            pv = c if pv is None else pv + c
        acc = acc * _by_head(qmask, alpha) + pv
        return tuple(m_new), tuple(l_new), acc

    init = (tuple(jnp.full((tq, 1), NEG_BIG, f32) for _ in range(HEADS)),
            tuple(jnp.zeros((tq, 1), f32) for _ in range(HEADS)),
            jnp.zeros((tq, ATT_W), f32))
    carry = lax.fori_loop(0, i, functools.partial(block, diag=False), init)
    _, l, acc = block(i, carry, diag=True)
    o_ref[...] = (acc / _by_head(qmask, l)).astype(bf16)


def _fox(q, kt, v, p, pt):
    bsz, s, _ = q.shape
    tq = ATT_TQ
    return pl.pallas_call(
        _fox_kernel,
        grid=(bsz, s // tq),
        in_specs=[pl.BlockSpec((None, tq, ATT_W), lambda b, i: (b, i, 0)),
                  pl.BlockSpec((None, ATT_W, s), lambda b, i: (b, 0, 0)),
                  pl.BlockSpec((None, s, ATT_W), lambda b, i: (b, 0, 0)),
                  pl.BlockSpec((None, tq, SMALL_W), lambda b, i: (b, i, 0)),
                  pl.BlockSpec((None, 16, s), lambda b, i: (b, 0, 0))],
        out_specs=pl.BlockSpec((None, tq, ATT_W), lambda b, i: (b, i, 0)),
        out_shape=jax.ShapeDtypeStruct((bsz, s, ATT_W), bf16),
        compiler_params=_params(("parallel", "parallel")),
        name="forgetting_attention",
    )(q, kt, v, p, pt)


def _sb_kernel(q_ref, kt_ref, v_ref, o_ref):
    tq, tk = ATT_TQ, ATT_TK
    i = pl.program_id(1)
    q = q_ref[...]
    qmask = _head_masks((tq, ATT_W), 1)
    vmask = _head_masks((tk, ATT_W), 1)
    zero = jnp.zeros((), bf16)
    qm = [jnp.where(qmask[hd], q, zero) for hd in range(HEADS)]
    rows = lax.broadcasted_iota(jnp.int32, (tq, tk), 0)
    cols = lax.broadcasted_iota(jnp.int32, (tq, tk), 1)
    strict = cols < rows
    kr = lax.broadcasted_iota(jnp.int32, (tk, tk), 0)
    kc = lax.broadcasted_iota(jnp.int32, (tk, tk), 1)
    later = jnp.where(kr > kc, 1.0, 0.0).astype(bf16)

    def block(j, rem, acc, diag):
        off = pl.multiple_of(j * tk, tk)
        kt = kt_ref[:, pl.ds(off, tk)]
        v = v_ref[pl.ds(off, tk), :]
        rem_new = []
        for hd in range(HEADS):
            z = _dot(qm[hd], kt)
            log_beta = jnp.minimum(z, 0.0) - _softplus_neg_abs(z)
            log_rem = log_beta - z
            if diag:
                log_rem = jnp.where(strict, log_rem, 0.0)
            hi, lo = _split2(log_rem)
            after = _dot(hi, later) + _dot(lo, later) + rem[hd]
            w = jnp.exp(log_beta + after)
            if diag:
                w = jnp.where(strict, w, 0.0)
            rem_new.append(rem[hd] + jnp.sum(log_rem, axis=-1, keepdims=True))
            acc = acc + _dot(w.astype(bf16), jnp.where(vmask[hd], v, zero))
        return tuple(rem_new), acc

    rem, acc = block(i, tuple(jnp.zeros((tq, 1), f32) for _ in range(HEADS)),
                     jnp.zeros((tq, ATT_W), f32), diag=True)

    def live(rem):
        top = rem[0]
        for hd in range(1, HEADS):
            top = jnp.maximum(top, rem[hd])
        return jnp.max(top) > EXP_UNDERFLOW

    def cond(state):
        return jnp.logical_and(state[0] >= 0, state[1])

    def body(state):
        j, _, rem, acc = state
        rem, acc = block(j, rem, acc, diag=False)
        return j - 1, live(rem), rem, acc

    _, _, _, acc = lax.while_loop(cond, body, (i - 1, live(rem), rem, acc))
    o_ref[...] = acc.astype(bf16)


def _sb(q, kt, v):
    bsz, s, _ = q.shape
    tq = ATT_TQ
    return pl.pallas_call(
        _sb_kernel,
        grid=(bsz, s // tq),
        in_specs=[pl.BlockSpec((None, tq, ATT_W), lambda b, i: (b, i, 0)),
                  pl.BlockSpec((None, ATT_W, s), lambda b, i: (b, 0, 0)),
                  pl.BlockSpec((None, s, ATT_W), lambda b, i: (b, 0, 0))],
        out_specs=pl.BlockSpec((None, tq, ATT_W), lambda b, i: (b, i, 0)),
        out_shape=jax.ShapeDtypeStruct((bsz, s, ATT_W), bf16),
        compiler_params=_params(("parallel", "parallel")),
        name="stick_breaking_attention",
    )(q, kt, v)


def _gdn_kernel(q_ref, k_ref, v_ref, p_ref, pt_ref, ng_ref, o_ref,
                qe_ref, ol_ref, g_ref, b_ref, dec_ref, state_ref):
    ts = q_ref.shape[0]
    sb = GDN_SB
    hs = range(HEADS)
    lanes = [slice(hd * GDN_HEAD, (hd + 1) * GDN_HEAD) for hd in hs]
    ri = lax.broadcasted_iota(jnp.int32, (sb, sb), 0)
    ci = lax.broadcasted_iota(jnp.int32, (sb, sb), 1)
    same = (ri // CHUNK) == (ci // CHUNK)
    incl = same & (ri >= ci)
    strict = same & (ri > ci)
    cast = lambda t: t.astype(bf16)

    @pl.when(pl.program_id(1) == 0)
    def _():
        state_ref[...] = jnp.zeros(state_ref.shape, f32)

    def local(i, _):
        rows = pl.ds(pl.multiple_of(i * sb, sb), sb)
        k = [k_ref[rows, lanes[h]] for h in hs]
        beta = [p_ref[rows, 4 + h:5 + h] for h in hs]
        gc = [p_ref[rows, 8 + h:9 + h] for h in hs]
        gl = [p_ref[rows, 12 + h:13 + h] for h in hs]
        decay = [jnp.where(incl, jnp.exp(jnp.where(incl, gc[h] - pt_ref[8 + h:9 + h, rows], 0.0)), 0.0)
                 for h in hs]
        kb = [k[h] * beta[h] for h in hs]
        kbf = [cast(k[h]) for h in hs]
        a = [jnp.where(strict, _dot_nt(cast(kb[h]), kbf[h]) * decay[h], 0.0) for h in hs]
        ab = [cast(a[h]) for h in hs]
        pw = [_dot(ab[h], ab[h]) for h in hs]
        n = [-a[h] for h in hs]
        for level in range(5):
            pb = [cast(pw[h]) for h in hs]
            n = [n[h] + pw[h] + _dot(cast(n[h]), pb[h]) for h in hs]
            if level < 4:
                pw = [_dot(pb[h], pb[h]) for h in hs]
        e_gc = [jnp.exp(gc[h]) for h in hs]
        rhs = [jnp.concatenate([v_ref[rows, lanes[h]] * beta[h], kb[h] * e_gc[h]], axis=1) for h in hs]
        sol = [rhs[h] + _dot(cast(n[h]), cast(rhs[h])) for h in hs]
        solb = [cast(sol[h]) for h in hs]
        q = [q_ref[rows, lanes[h]] for h in hs]
        intra = [cast(jnp.where(incl, _dot_nt(cast(q[h]), kbf[h]) * decay[h], 0.0)) for h in hs]
        prod = [_dot(intra[h], solb[h]) for h in hs]
        kd = [cast(k[h] * jnp.exp(gl[h] - gc[h])) for h in hs]
        for h in hs:
            ol_ref[rows, lanes[h]] = prod[h][:, :GDN_HEAD]
            qe_ref[rows, lanes[h]] = cast(q[h] * e_gc[h] - prod[h][:, GDN_HEAD:])
            glb = jnp.exp(jnp.broadcast_to(gl[h], (sb, GDN_HEAD)))
            for c in range(2):
                crow = slice(c * CHUNK, (c + 1) * CHUNK)
                bg = _dot_tn(kd[h][crow], solb[h][crow])
                grow = pl.ds(pl.multiple_of((2 * i + c) * GDN_HEAD, GDN_HEAD), GDN_HEAD)
                b_ref[h, grow, :] = bg[:, :GDN_HEAD]
                g_ref[h, grow, :] = cast(bg[:, GDN_HEAD:])
                dec_ref[h, pl.ds(2 * i + c, 1), :] = glb[c * CHUNK:c * CHUNK + 1, :]
        return 0

    lax.fori_loop(0, ts // sb, local, 0)

    def scan(c, _):
        crow = pl.ds(pl.multiple_of(c * CHUNK, CHUNK), CHUNK)
        grow = pl.ds(pl.multiple_of(c * GDN_HEAD, GDN_HEAD), GDN_HEAD)
        state = [state_ref[h] for h in hs]
        sbf = [cast(state[h]) for h in hs]
        gs = [_dot(g_ref[h, grow, :], sbf[h]) for h in hs]
        for h in hs:
            state_ref[h] = state[h] * dec_ref[h, pl.ds(c, 1), :] - gs[h] + b_ref[h, grow, :]
        o = [_dot(qe_ref[crow, lanes[h]], sbf[h]) + ol_ref[crow, lanes[h]] for h in hs]
        for h in hs:
            o_ref[crow, lanes[h]] = (o[h] * lax.rsqrt(jnp.mean(o[h] * o[h], axis=-1, keepdims=True)
                                                      + RMS_EPS) * ng_ref[...])
        return 0

    lax.fori_loop(0, ts // CHUNK, scan, 0)


def _gdn(q, k, v, p, pt, ng):
    bsz, s, _ = q.shape
    ts = min(GDN_TS, s)
    n_chunks = ts // CHUNK
    row = lambda w: pl.BlockSpec((None, ts, w), lambda b, i: (b, i, 0))
    return pl.pallas_call(
        _gdn_kernel,
        grid=(bsz, s // ts),
        in_specs=[row(GDN_W), row(GDN_W), row(GDN_W), row(SMALL_W),
                  pl.BlockSpec((None, 16, ts), lambda b, i: (b, 0, i)), _const_spec(ng.shape)],
        out_specs=row(GDN_W),
        out_shape=jax.ShapeDtypeStruct((bsz, s, GDN_W), f32),
        scratch_shapes=[pltpu.VMEM((ts, GDN_W), bf16), pltpu.VMEM((ts, GDN_W), f32),
                        pltpu.VMEM((HEADS, n_chunks * GDN_HEAD, GDN_HEAD), bf16),
                        pltpu.VMEM((HEADS, n_chunks * GDN_HEAD, GDN_HEAD), f32),
                        pltpu.VMEM((HEADS, n_chunks, GDN_HEAD), f32),
                        pltpu.VMEM((HEADS, GDN_HEAD, GDN_HEAD), f32)],
        compiler_params=_params(("parallel", "arbitrary")),
        name="gated_delta_rule",
    )(q, k, v, p, pt, ng)


def _merge_kernel(x_ref, mod_ref, oa_ref, ob_ref, on_ref, wgate_ref, wgg_ref,
                  wba_ref, wbb_ref, wbc_ref, wo_ref, lng_ref, lnb_ref, o_ref, h_ref):
    x = x_ref[...]
    d = x.shape[1]
    h_ref[...] = (x * (1.0 + mod_ref[1:2, :]) + mod_ref[0:1, :]).astype(bf16)
    gg = _dot(h_ref[...], wgg_ref[...])
    oc = (on_ref[...] * (gg * jax.nn.sigmoid(gg))).astype(bf16)
    branches = ((oa_ref[...], wba_ref), (ob_ref[...], wbb_ref), (oc, wbc_ref))
    merged = None
    for n, (o, wb_ref) in enumerate(branches):
        gate = jax.nn.sigmoid(_dot(h_ref[...], wgate_ref[:, n * d:(n + 1) * d]))
        term = gate * _dot(o, wb_ref[...])
        merged = term if merged is None else merged + term
    y = _dot(merged.astype(bf16), wo_ref[...])
    r = DEEPNORM_ALPHA * x + (1.0 + mod_ref[2:3, :]) * y
    o_ref[...] = _layer_norm(r, lng_ref[...], lnb_ref[...])


def _merge(x, mod, oa, ob, on, wgate, wgg, wba, wbb, wbc, wo, lng, lnb):
    bsz, s, d = x.shape
    tm = MERGE_TM
    row = lambda w: pl.BlockSpec((None, tm, w), lambda b, i: (b, i, 0))
    return pl.pallas_call(
        _merge_kernel,
        grid=(bsz, s // tm),
        in_specs=[row(d), pl.BlockSpec((None, 3, d), lambda b, i: (b, 0, 0)),
                  row(ATT_W), row(ATT_W), row(GDN_W),
                  _const_spec(wgate.shape), _const_spec(wgg.shape), _const_spec(wba.shape),
                  _const_spec(wbb.shape), _const_spec(wbc.shape), _const_spec(wo.shape),
                  _const_spec((1, d)), _const_spec((1, d))],
        out_specs=row(d),
        out_shape=jax.ShapeDtypeStruct(x.shape, f32),
        scratch_shapes=[pltpu.VMEM((tm, d), bf16)],
        compiler_params=_params(("parallel", "parallel")),
        name="mixer_merge",
    )(x, mod, oa, ob, on, wgate, wgg, wba, wbb, wbc, wo, lng, lnb)


def _mixer_weights(w_in, b_forget, a_log, dt_bias):
    d = w_in.shape[0]
    sizes = (ATT_W, ATT_W, ATT_W, HEADS, ATT_W, ATT_W, ATT_W, 3 * GDN_W, HEADS, HEADS, GDN_W,
             3 * D_MODEL)
    offs = [0]
    for n in sizes:
        offs.append(offs[-1] + n)
    piece = lambda n: w_in[:, offs[n]:offs[n + 1]]
    wa = jnp.concatenate([piece(0), piece(1), piece(2)], axis=1).astype(bf16)
    wb = jnp.concatenate([piece(4), piece(5), piece(6)], axis=1).astype(bf16)
    wc = piece(7).astype(bf16)
    ws = jnp.concatenate([piece(3), piece(8), piece(9), piece(9),
                          jnp.zeros((d, SMALL_W - 4 * HEADS), w_in.dtype)], axis=1).astype(bf16)
    wgg = piece(10).astype(bf16)
    wgate = piece(11).astype(bf16)
    pad = lambda vec, at: jnp.zeros((SMALL_W,), f32).at[at:at + HEADS].set(vec.astype(f32))
    pv = jnp.stack([pad(b_forget, 0),
                    pad(a_log, 2 * HEADS) + pad(a_log, 3 * HEADS),
                    pad(dt_bias, 2 * HEADS) + pad(dt_bias, 3 * HEADS)]
                   + [jnp.zeros((SMALL_W,), f32)] * 5)
    return wa, wb, wc, ws, wgg, wgate, pv


def kernel(x, c, w_ada, b_ada, ln_g, ln_b, ffn_w_up, ffn_w_down, w_in, b_forget, conv_w,
           gdn_a_log, gdn_dt_bias, gdn_norm_g, w_branch, w_o):
    bsz = x.shape[0]
    mod = _ada(c, w_ada, b_ada).reshape(DEPTH, bsz, N_SUB, 3, D_MODEL)
    wup = ffn_w_up.astype(bf16)
    wdown = ffn_w_down.astype(bf16)
    for l in range(DEPTH):
        lng = lambda n: ln_g[l, n].reshape(1, D_MODEL)
        lnb = lambda n: ln_b[l, n].reshape(1, D_MODEL)
        x = _ffn(x, mod[l, :, 0], wup[l, 0], wdown[l, 0], lng(0), lnb(0))

        wa, wb, wc, ws, wgg, wgate, pv = _mixer_weights(w_in[l], b_forget[l], gdn_a_log[l],
                                                        gdn_dt_bias[l])
        m = mod[l, :, 1]
        fq, fkt, fv, sq, skt, sv, gq, gk, gv, sm = _proj(x, m, wa, wb, wc, ws, conv_w[l])
        p, pt = _prefix(sm, pv)
        oa = _fox(fq, fkt, fv, p, pt)
        ob = _sb(sq, skt, sv)
        on = _gdn(gq, gk, gv, p, pt, gdn_norm_g[l].reshape(1, GDN_HEAD))
        wbr = w_branch[l].astype(bf16)
        x = _merge(x, m, oa, ob, on, wgate, wgg, wbr[:ATT_W], wbr[ATT_W:2 * ATT_W],
                   wbr[2 * ATT_W:], w_o[l].astype(bf16), lng(1), lnb(1))

        x = _ffn(x, mod[l, :, 2], wup[l, 1], wdown[l, 1], lng(2), lnb(2))
    return x
```

```python
import functools

import jax
import jax.numpy as jnp
from jax import lax
from jax.experimental import pallas as pl
from jax.experimental.pallas import tpu as pltpu

f32 = jnp.float32
bf16 = jnp.bfloat16

D_MODEL = 1024
DEPTH = 2
N_SUB = 3
FFN_HIDDEN = 2816
HEADS = 4
ATT_W = 256
ATT_HEAD = 64
GDN_W = 512
GDN_HEAD = 128
CHUNK = 64
CONV_WIDTH = 4
SMALL_W = 128
DEEPNORM_ALPHA = (2.0 * DEPTH) ** 0.25
LN_EPS = 1e-5
RMS_EPS = 1e-6
LOG2E = 1.4426950408889634

FFN_TM = 512
FFN_HC = 256
PROJ_TM = 512
MERGE_TM = 512
LANES = 128
ATT_TQ = 256
ATT_TK = 256
FOX_TQ = 512
FOX_TK = 512
PREFIX_BLK = 256
GDN_SB = 128
GDN_TS = 1024
CONV_HALO = 8
NEG_BIG = -1e30
EXP_UNDERFLOW = -104.0
VMEM_LIMIT = 56 * 1024 * 1024


def _dot(a, b):
    return jnp.dot(a, b, preferred_element_type=f32)


def _dot_nt(a, b):
    return lax.dot_general(a, b, (((1,), (1,)), ((), ())), preferred_element_type=f32)


def _dot_tn(a, b):
    return lax.dot_general(a, b, (((0,), (0,)), ((), ())), preferred_element_type=f32)


def _split2(x):
    hi = x.astype(bf16)
    lo = (x - hi.astype(f32)).astype(bf16)
    return hi, lo


def _split3(x):
    hi = x.astype(bf16)
    r = x - hi.astype(f32)
    mid = r.astype(bf16)
    lo = (r - mid.astype(f32)).astype(bf16)
    return hi, mid, lo


def _dot3(a, b):
    ah, al = _split2(a)
    bh, bl = _split2(b)
    return _dot(ah, bh) + (_dot(ah, bl) + _dot(al, bh))


def _layer_norm(r, g, b):
    mu = jnp.mean(r, axis=-1, keepdims=True)
    d = r - mu
    var = jnp.mean(d * d, axis=-1, keepdims=True)
    return d * lax.rsqrt(var + LN_EPS) * g + b


def _softplus_neg_abs(x):
    return jnp.log1p(jnp.exp(-jnp.abs(x)))


def _const_spec(shape):
    nd = len(shape)
    return pl.BlockSpec(shape, lambda *_: (0,) * nd, pipeline_mode=pl.Buffered(1))


def _params(sem):
    return pltpu.CompilerParams(dimension_semantics=sem, vmem_limit_bytes=VMEM_LIMIT)


def _ada_kernel(c_ref, w_ref, b_ref, o_ref):
    c = c_ref[...]
    sc = c * jax.nn.sigmoid(c)
    o_ref[...] = _dot3(sc, w_ref[...]) + b_ref[...]


def _ada(c, w_ada, b_ada):
    depth, d, n = w_ada.shape
    bsz = c.shape[0]
    tn = 1024
    return pl.pallas_call(
        _ada_kernel,
        grid=(depth, n // tn),
        in_specs=[pl.BlockSpec((bsz, d), lambda l, j: (0, 0)),
                  pl.BlockSpec((None, d, tn), lambda l, j: (l, 0, j)),
                  pl.BlockSpec((None, 1, tn), lambda l, j: (l, 0, j))],
        out_specs=pl.BlockSpec((None, bsz, tn), lambda l, j: (l, 0, j)),
        out_shape=jax.ShapeDtypeStruct((depth, bsz, n), f32),
        compiler_params=_params(("parallel", "parallel")),
        name="adaln",
    )(c, w_ada, b_ada.reshape(depth, 1, n))


def _ffn_kernel(x_ref, mod_ref, wup_ref, wd_ref, lng_ref, lnb_ref, o_ref, h_ref, acc_ref):
    x = x_ref[...]
    h_ref[...] = (x * (1.0 + mod_ref[1:2, :]) + mod_ref[0:1, :]).astype(bf16)
    for j in range(FFN_HIDDEN // FFN_HC):
        lo = j * FFN_HC
        h = h_ref[...]
        g = _dot(h, wup_ref[:, lo:lo + FFN_HC])
        u = _dot(h, wup_ref[:, FFN_HIDDEN + lo:FFN_HIDDEN + lo + FFN_HC])
        a = (g * jax.nn.sigmoid(g) * u).astype(bf16)
        y = _dot(a, wd_ref[lo:lo + FFN_HC, :])
        if j == 0:
            acc_ref[...] = y
        else:
            acc_ref[...] += y
    r = DEEPNORM_ALPHA * x + (0.5 * (1.0 + mod_ref[2:3, :])) * acc_ref[...]
    o_ref[...] = _layer_norm(r, lng_ref[...], lnb_ref[...])


def _ffn(x, mod, wup, wd, lng, lnb):
    bsz, s, d = x.shape
    tm = FFN_TM
    return pl.pallas_call(
        _ffn_kernel,
        grid=(bsz, s // tm),
        in_specs=[pl.BlockSpec((None, tm, d), lambda b, i: (b, i, 0)),
                  pl.BlockSpec((None, 3, d), lambda b, i: (b, 0, 0)),
                  _const_spec(wup.shape), _const_spec(wd.shape),
                  _const_spec((1, d)), _const_spec((1, d))],
        out_specs=pl.BlockSpec((None, tm, d), lambda b, i: (b, i, 0)),
        out_shape=jax.ShapeDtypeStruct(x.shape, f32),
        scratch_shapes=[pltpu.VMEM((tm, d), bf16), pltpu.VMEM((tm, d), f32)],
        compiler_params=_params(("parallel", "parallel")),
        name="ffn",
    )(x, mod, wup, wd, lng, lnb)


def _proj_kernel(x_ref, mod_ref, wa_ref, wb_ref, wc_ref, ws_ref, cw_ref,
                 fq_ref, fkt_ref, fv_ref, sq_ref, skt_ref, sv_ref,
                 gq_ref, gk_ref, gv_ref, sm_ref, h_ref, cbuf_ref):
    tm = x_ref.shape[0]
    h_ref[...] = (x_ref[...] * (1.0 + mod_ref[1:2, :]) + mod_ref[0:1, :]).astype(bf16)

    def attn_group(w_ref, q_ref, kt_ref, v_ref, q_scale):
        r = _dot(h_ref[...], w_ref[...])
        q_ref[...] = (r[:, :ATT_W] * q_scale).astype(bf16)
        kt_ref[...] = r[:, ATT_W:2 * ATT_W].T.astype(bf16)
        v_ref[...] = r[:, 2 * ATT_W:].astype(bf16)

    @pl.when(pl.program_id(1) == 0)
    def _():
        cbuf_ref[0:CONV_HALO, :] = jnp.zeros((CONV_HALO, cbuf_ref.shape[1]), f32)

    def conv_act(part):
        cols = slice(part * GDN_W, (part + 1) * GDN_W)
        cbuf_ref[CONV_HALO:, cols] = _dot(h_ref[...], wc_ref[:, cols])
        conv = cbuf_ref[CONV_HALO:, cols] * cw_ref[CONV_WIDTH - 1:CONV_WIDTH, cols]
        for tap in range(CONV_WIDTH - 1):
            back = CONV_WIDTH - 1 - tap
            conv = conv + (cbuf_ref[CONV_HALO - back:CONV_HALO - back + tm, cols]
                           * cw_ref[tap:tap + 1, cols])
        cbuf_ref[0:CONV_HALO, cols] = cbuf_ref[tm:tm + CONV_HALO, cols]
        return conv * jax.nn.sigmoid(conv)

    def l2norm_heads(act, o_ref, scale):
        for hd in range(HEADS):
            lanes = slice(hd * GDN_HEAD, (hd + 1) * GDN_HEAD)
            t = act[:, lanes]
            o_ref[:, lanes] = t * lax.rsqrt(jnp.sum(t * t, -1, keepdims=True) + RMS_EPS) * scale

    l2norm_heads(conv_act(0), gq_ref, GDN_HEAD ** -0.5)
    attn_group(wa_ref, fq_ref, fkt_ref, fv_ref, LOG2E * ATT_HEAD ** -0.5)
    l2norm_heads(conv_act(1), gk_ref, 1.0)
    attn_group(wb_ref, sq_ref, skt_ref, sv_ref, ATT_HEAD ** -0.5)
    gv_ref[...] = conv_act(2)
    sm_ref[...] = _dot(h_ref[...], ws_ref[...])


def _proj(x, mod, wa, wb, wc, ws, cw):
    bsz, s, d = x.shape
    tm = PROJ_TM
    row = lambda w: pl.BlockSpec((None, tm, w), lambda b, i: (b, i, 0))
    col = pl.BlockSpec((None, ATT_W, tm), lambda b, i: (b, 0, i))
    sd = jax.ShapeDtypeStruct
    return pl.pallas_call(
        _proj_kernel,
        grid=(bsz, s // tm),
        in_specs=[row(d), pl.BlockSpec((None, 3, d), lambda b, i: (b, 0, 0)),
                  _const_spec(wa.shape), _const_spec(wb.shape), _const_spec(wc.shape),
                  _const_spec(ws.shape), _const_spec(cw.shape)],
        out_specs=[row(ATT_W), col, row(ATT_W), row(ATT_W), col, row(ATT_W),
                   row(GDN_W), row(GDN_W), row(GDN_W), row(SMALL_W)],
        out_shape=[sd((bsz, s, ATT_W), bf16), sd((bsz, ATT_W, s), bf16), sd((bsz, s, ATT_W), bf16),
                   sd((bsz, s, ATT_W), bf16), sd((bsz, ATT_W, s), bf16), sd((bsz, s, ATT_W), bf16),
                   sd((bsz, s, GDN_W), f32), sd((bsz, s, GDN_W), f32), sd((bsz, s, GDN_W), f32),
                   sd((bsz, s, SMALL_W), f32)],
        scratch_shapes=[pltpu.VMEM((tm, d), bf16), pltpu.VMEM((tm + CONV_HALO, 3 * GDN_W), f32)],
        compiler_params=_params(("parallel", "arbitrary")),
        name="mixer_in_proj",
    )(x, mod, wa, wb, wc, ws, cw)


def _prefix_kernel(sm_ref, pv_ref, p_ref, pt_ref):
    s = sm_ref.shape[0]
    blk = PREFIX_BLK
    ri = lax.broadcasted_iota(jnp.int32, (blk, blk), 0)
    ci = lax.broadcasted_iota(jnp.int32, (blk, blk), 1)
    low = ri >= ci
    same = (ri // CHUNK) == (ci // CHUNK)
    tri_full = jnp.where(low, 1.0, 0.0).astype(bf16)
    tri_chunk = jnp.where(low & same, 1.0, 0.0).astype(bf16)
    ones_chunk = jnp.where(same, 1.0, 0.0).astype(bf16)
    lane = lax.broadcasted_iota(jnp.int32, (blk, SMALL_W), 1)
    b_forget = pv_ref[0:1, :]
    neg_a = -jnp.exp(pv_ref[1:2, :])
    dt_bias = pv_ref[2:3, :]
    carry = jnp.zeros((1, SMALL_W), f32)
    for i in range(s // blk):
        x = sm_ref[i * blk:(i + 1) * blk, :]
        xf = x + b_forget
        log_f = jnp.minimum(xf, 0.0) - _softplus_neg_abs(xf)
        beta = jax.nn.sigmoid(x)
        xg = x + dt_bias
        g = neg_a * (jnp.maximum(xg, 0.0) + _softplus_neg_abs(xg))
        val = jnp.where(lane < 4, log_f, jnp.where(lane < 8, beta, jnp.where(lane < 16, g, 0.0)))
        parts = _split3(val)
        cum_full = sum(_dot(tri_full, p) for p in parts) + carry
        cum_chunk = sum(_dot(tri_chunk, p) for p in parts)
        tot_chunk = sum(_dot(ones_chunk, p) for p in parts)
        carry = cum_full[blk - 1:blk, :]
        out = jnp.where(lane < 4, cum_full,
                        jnp.where(lane < 8, val, jnp.where(lane < 12, cum_chunk, tot_chunk)))
        p_ref[i * blk:(i + 1) * blk, :] = out
        pt_ref[:, i * blk:(i + 1) * blk] = out.T[0:16, :]


def _prefix(sm, pv):
    bsz, s, _ = sm.shape
    return pl.pallas_call(
        _prefix_kernel,
        grid=(bsz,),
        in_specs=[pl.BlockSpec((None, s, SMALL_W), lambda b: (b, 0, 0)), _const_spec(pv.shape)],
        out_specs=[pl.BlockSpec((None, s, SMALL_W), lambda b: (b, 0, 0)),
                   pl.BlockSpec((None, 16, s), lambda b: (b, 0, 0))],
        out_shape=[jax.ShapeDtypeStruct((bsz, s, SMALL_W), f32),
                   jax.ShapeDtypeStruct((bsz, 16, s), f32)],
        compiler_params=_params(("parallel",)),
        name="gate_prefix",
    )(sm, pv)


def _head_masks(shape, axis):
    lane = lax.broadcasted_iota(jnp.int32, shape, axis)
    return [(lane >= hd * ATT_HEAD) & (lane < (hd + 1) * ATT_HEAD) for hd in range(HEADS)]


def _by_head(masks, cols):
    out = cols[HEADS - 1]
    for hd in range(HEADS - 2, -1, -1):
        out = jnp.where(masks[hd], cols[hd], out)
    return out


def _fox_kernel(q_ref, kt_ref, v_ref, p_ref, pt_ref, o_ref,
                vm_ref, s_ref, cq_ref, m_ref, l_ref, acc_ref):
    tq, tk, lw = FOX_TQ, FOX_TK, LANES
    hs = range(HEADS)
    groups = range(tk // lw)
    i = pl.program_id(1)
    q = q_ref[...]
    qmask = _head_masks((tq, ATT_W), 1)
    zero = jnp.zeros((), bf16)
    qm = [jnp.where(qmask[hd], q, zero) for hd in hs]

    @pl.when(i == 0)
    def _():
        vmask = _head_masks(v_ref.shape, 1)
        v = v_ref[...]
        for hd in hs:
            vm_ref[hd] = jnp.where(vmask[hd], v, zero)

    delta = (lax.broadcasted_iota(jnp.int32, (tq, lw), 1)
             - lax.broadcasted_iota(jnp.int32, (tq, lw), 0))
    n_full = (i * tq) // tk

    def score_block(j, _, masked=False):
        off = pl.multiple_of(j * tk, tk)
        kt = kt_ref[:, pl.ds(off, tk)]
        for hd in hs:
            s = _dot(qm[hd], kt)
            cq = cq_ref[hd]
            ck = pt_ref[hd:hd + 1, pl.ds(off, tk)] * LOG2E
            mx = m_ref[hd]
            for g in groups:
                sg = s[:, g * lw:(g + 1) * lw] + cq - ck[:, g * lw:(g + 1) * lw]
                if masked:
                    sg = jnp.where(delta <= i * tq - off - g * lw, sg, NEG_BIG)
                s_ref[hd, :, pl.ds(off + g * lw, lw)] = sg
                mx = jnp.maximum(mx, sg)
            m_ref[hd] = mx
        return 0

    m_ref[...] = jnp.full(m_ref.shape, NEG_BIG, f32)
    for hd in hs:
        cq_ref[hd] = jnp.broadcast_to(p_ref[:, hd:hd + 1], (tq, lw)) * LOG2E
    lax.fori_loop(0, n_full, score_block, 0)
    score_block(n_full, 0, masked=True)
    for hd in hs:
        m_ref[hd] = jnp.broadcast_to(jnp.max(m_ref[hd], axis=-1, keepdims=True), (tq, lw))

    def sum_block(j, _):
        off = pl.multiple_of(j * tk, tk)
        pv = None
        for hd in hs:
            mx = m_ref[hd]
            p = jnp.exp2(s_ref[hd, :, pl.ds(off, tk)] - jnp.concatenate([mx] * len(groups), axis=1))
            l = l_ref[hd]
            for g in groups:
                l = l + p[:, g * lw:(g + 1) * lw]
            l_ref[hd] = l
            c = _dot(p.astype(bf16), vm_ref[hd, pl.ds(off, tk), :])
            pv = c if pv is None else pv + c
        acc_ref[...] += pv
        return 0

    l_ref[...] = jnp.zeros(l_ref.shape, f32)
    acc_ref[...] = jnp.zeros(acc_ref.shape, f32)
    lax.fori_loop(0, n_full + 1, sum_block, 0)
    l = [jnp.sum(l_ref[hd], axis=-1, keepdims=True) for hd in hs]
    o_ref[...] = (acc_ref[...] / _by_head(qmask, l)).astype(bf16)


def _fox(q, kt, v, p, pt):
    bsz, s, _ = q.shape
    tq = FOX_TQ
    return pl.pallas_call(
        _fox_kernel,
        grid=(bsz, s // tq),
        in_specs=[pl.BlockSpec((None, tq, ATT_W), lambda b, i: (b, i, 0)),
                  pl.BlockSpec((None, ATT_W, s), lambda b, i: (b, 0, 0)),
                  pl.BlockSpec((None, s, ATT_W), lambda b, i: (b, 0, 0)),
                  pl.BlockSpec((None, tq, SMALL_W), lambda b, i: (b, i, 0)),
                  pl.BlockSpec((None, 16, s), lambda b, i: (b, 0, 0))],
        out_specs=pl.BlockSpec((None, tq, ATT_W), lambda b, i: (b, i, 0)),
        out_shape=jax.ShapeDtypeStruct((bsz, s, ATT_W), bf16),
        scratch_shapes=[pltpu.VMEM((HEADS, s, ATT_W), bf16),
                        pltpu.VMEM((HEADS, tq, s), f32),
                        pltpu.VMEM((HEADS, tq, LANES), f32),
                        pltpu.VMEM((HEADS, tq, LANES), f32),
                        pltpu.VMEM((HEADS, tq, LANES), f32),
                        pltpu.VMEM((tq, ATT_W), f32)],
        compiler_params=_params(("parallel", "arbitrary")),
        name="forgetting_attention",
    )(q, kt, v, p, pt)


def _sb_kernel(q_ref, kt_ref, v_ref, o_ref):
    tq, tk = ATT_TQ, ATT_TK
    i = pl.program_id(1)
    q = q_ref[...]
    qmask = _head_masks((tq, ATT_W), 1)
    vmask = _head_masks((tk, ATT_W), 1)
    zero = jnp.zeros((), bf16)
    qm = [jnp.where(qmask[hd], q, zero) for hd in range(HEADS)]
    rows = lax.broadcasted_iota(jnp.int32, (tq, tk), 0)
    cols = lax.broadcasted_iota(jnp.int32, (tq, tk), 1)
    strict = cols < rows
    kr = lax.broadcasted_iota(jnp.int32, (tk, tk), 0)
    kc = lax.broadcasted_iota(jnp.int32, (tk, tk), 1)
    later = jnp.where(kr > kc, 1.0, 0.0).astype(bf16)

    def block(j, rem, acc, diag):
        off = pl.multiple_of(j * tk, tk)
        kt = kt_ref[:, pl.ds(off, tk)]
        v = v_ref[pl.ds(off, tk), :]
        hs = range(HEADS)
        z = [_dot(qm[hd], kt) for hd in hs]
        log_beta = [jnp.minimum(z[hd], 0.0) - jnp.log(1.0 + jnp.exp(-jnp.abs(z[hd])))
                    for hd in hs]
        log_rem = [log_beta[hd] - z[hd] for hd in hs]
        if diag:
            log_rem = [jnp.where(strict, log_rem[hd], 0.0) for hd in hs]
        parts = [_split2(log_rem[hd]) for hd in hs]
        after = [_dot(parts[hd][0], later) + _dot(parts[hd][1], later) + rem[hd] for hd in hs]
        w = [jnp.exp(log_beta[hd] + after[hd]) for hd in hs]
        if diag:
            w = [jnp.where(strict, w[hd], 0.0) for hd in hs]
        rem_new = [rem[hd] + jnp.sum(log_rem[hd], axis=-1, keepdims=True) for hd in hs]
        for hd in hs:
            acc = acc + _dot(w[hd].astype(bf16), jnp.where(vmask[hd], v, zero))
        return tuple(rem_new), acc

    rem, acc = block(i, tuple(jnp.zeros((tq, 1), f32) for _ in range(HEADS)),
                     jnp.zeros((tq, ATT_W), f32), diag=True)

    def live(rem):
        top = rem[0]
        for hd in range(1, HEADS):
            top = jnp.maximum(top, rem[hd])
        return jnp.max(top) > EXP_UNDERFLOW

    def cond(state):
        return jnp.logical_and(state[0] >= 0, state[1])

    def body(state):
        j, _, rem, acc = state
        rem, acc = block(j, rem, acc, diag=False)
        return j - 1, live(rem), rem, acc

    _, _, _, acc = lax.while_loop(cond, body, (i - 1, live(rem), rem, acc))
    o_ref[...] = acc.astype(bf16)


def _sb(q, kt, v):
    bsz, s, _ = q.shape
    tq = ATT_TQ
    return pl.pallas_call(
        _sb_kernel,
        grid=(bsz, s // tq),
        in_specs=[pl.BlockSpec((None, tq, ATT_W), lambda b, i: (b, i, 0)),
                  pl.BlockSpec((None, ATT_W, s), lambda b, i: (b, 0, 0)),
                  pl.BlockSpec((None, s, ATT_W), lambda b, i: (b, 0, 0))],
        out_specs=pl.BlockSpec((None, tq, ATT_W), lambda b, i: (b, i, 0)),
        out_shape=jax.ShapeDtypeStruct((bsz, s, ATT_W), bf16),
        compiler_params=_params(("parallel", "parallel")),
        name="stick_breaking_attention",
    )(q, kt, v)


def _gdn_kernel(q_ref, k_ref, v_ref, p_ref, pt_ref, ng_ref, o_ref,
                qe_ref, ol_ref, g_ref, b_ref, dec_ref, state_ref):
    ts = q_ref.shape[0]
    sb = GDN_SB
    hs = range(HEADS)
    lanes = [slice(hd * GDN_HEAD, (hd + 1) * GDN_HEAD) for hd in hs]
    ri = lax.broadcasted_iota(jnp.int32, (sb, sb), 0)
    ci = lax.broadcasted_iota(jnp.int32, (sb, sb), 1)
    same = (ri // CHUNK) == (ci // CHUNK)
    incl = same & (ri >= ci)
    strict = same & (ri > ci)
    cast = lambda t: t.astype(bf16)

    @pl.when(pl.program_id(1) == 0)
    def _():
        state_ref[...] = jnp.zeros(state_ref.shape, f32)

    def local(i, _):
        rows = pl.ds(pl.multiple_of(i * sb, sb), sb)
        k = [k_ref[rows, lanes[h]] for h in hs]
        beta = [p_ref[rows, 4 + h:5 + h] for h in hs]
        gc = [p_ref[rows, 8 + h:9 + h] for h in hs]
        gl = [p_ref[rows, 12 + h:13 + h] for h in hs]
        decay = [jnp.where(incl, jnp.exp(jnp.where(incl, gc[h] - pt_ref[8 + h:9 + h, rows], 0.0)), 0.0)
                 for h in hs]
        kb = [k[h] * beta[h] for h in hs]
        kbf = [cast(k[h]) for h in hs]
        a = [jnp.where(strict, _dot_nt(cast(kb[h]), kbf[h]) * decay[h], 0.0) for h in hs]
        ab = [cast(a[h]) for h in hs]
        pw = [_dot(ab[h], ab[h]) for h in hs]
        n = [-a[h] for h in hs]
        for level in range(5):
            pb = [cast(pw[h]) for h in hs]
            n = [n[h] + pw[h] + _dot(cast(n[h]), pb[h]) for h in hs]
            if level < 4:
                pw = [_dot(pb[h], pb[h]) for h in hs]
        e_gc = [jnp.exp(gc[h]) for h in hs]
        rhs = [jnp.concatenate([v_ref[rows, lanes[h]] * beta[h], kb[h] * e_gc[h]], axis=1) for h in hs]
        sol = [rhs[h] + _dot(cast(n[h]), cast(rhs[h])) for h in hs]
        solb = [cast(sol[h]) for h in hs]
        q = [q_ref[rows, lanes[h]] for h in hs]
        intra = [cast(jnp.where(incl, _dot_nt(cast(q[h]), kbf[h]) * decay[h], 0.0)) for h in hs]
        prod = [_dot(intra[h], solb[h]) for h in hs]
        kd = [cast(k[h] * jnp.exp(gl[h] - gc[h])) for h in hs]
        for h in hs:
            ol_ref[rows, lanes[h]] = prod[h][:, :GDN_HEAD]
            qe_ref[rows, lanes[h]] = cast(q[h] * e_gc[h] - prod[h][:, GDN_HEAD:])
            glb = jnp.exp(jnp.broadcast_to(gl[h], (sb, GDN_HEAD)))
            for c in range(2):
                crow = slice(c * CHUNK, (c + 1) * CHUNK)
                bg = _dot_tn(kd[h][crow], solb[h][crow])
                grow = pl.ds(pl.multiple_of((2 * i + c) * GDN_HEAD, GDN_HEAD), GDN_HEAD)
                b_ref[h, grow, :] = bg[:, :GDN_HEAD]
                g_ref[h, grow, :] = cast(bg[:, GDN_HEAD:])
                dec_ref[h, pl.ds(2 * i + c, 1), :] = glb[c * CHUNK:c * CHUNK + 1, :]
        return 0

    lax.fori_loop(0, ts // sb, local, 0)

    def scan(c, _):
        crow = pl.ds(pl.multiple_of(c * CHUNK, CHUNK), CHUNK)
        grow = pl.ds(pl.multiple_of(c * GDN_HEAD, GDN_HEAD), GDN_HEAD)
        state = [state_ref[h] for h in hs]
        sbf = [cast(state[h]) for h in hs]
        gs = [_dot(g_ref[h, grow, :], sbf[h]) for h in hs]
        for h in hs:
            state_ref[h] = state[h] * dec_ref[h, pl.ds(c, 1), :] - gs[h] + b_ref[h, grow, :]
        o = [_dot(qe_ref[crow, lanes[h]], sbf[h]) + ol_ref[crow, lanes[h]] for h in hs]
        for h in hs:
            o_ref[crow, lanes[h]] = (o[h] * lax.rsqrt(jnp.mean(o[h] * o[h], axis=-1, keepdims=True)
                                                      + RMS_EPS) * ng_ref[...])
        return 0

    lax.fori_loop(0, ts // CHUNK, scan, 0)


def _gdn(q, k, v, p, pt, ng):
    bsz, s, _ = q.shape
    ts = min(GDN_TS, s)
    n_chunks = ts // CHUNK
    row = lambda w: pl.BlockSpec((None, ts, w), lambda b, i: (b, i, 0))
    return pl.pallas_call(
        _gdn_kernel,
        grid=(bsz, s // ts),
        in_specs=[row(GDN_W), row(GDN_W), row(GDN_W), row(SMALL_W),
                  pl.BlockSpec((None, 16, ts), lambda b, i: (b, 0, i)), _const_spec(ng.shape)],
        out_specs=row(GDN_W),
        out_shape=jax.ShapeDtypeStruct((bsz, s, GDN_W), f32),
        scratch_shapes=[pltpu.VMEM((ts, GDN_W), bf16), pltpu.VMEM((ts, GDN_W), f32),
                        pltpu.VMEM((HEADS, n_chunks * GDN_HEAD, GDN_HEAD), bf16),
                        pltpu.VMEM((HEADS, n_chunks * GDN_HEAD, GDN_HEAD), f32),
                        pltpu.VMEM((HEADS, n_chunks, GDN_HEAD), f32),
                        pltpu.VMEM((HEADS, GDN_HEAD, GDN_HEAD), f32)],
        compiler_params=_params(("parallel", "arbitrary")),
        name="gated_delta_rule",
    )(q, k, v, p, pt, ng)


def _merge_kernel(x_ref, mod_ref, oa_ref, ob_ref, on_ref, wgate_ref, wgg_ref,
                  wba_ref, wbb_ref, wbc_ref, wo_ref, lng_ref, lnb_ref, o_ref, h_ref):
    x = x_ref[...]
    d = x.shape[1]
    h_ref[...] = (x * (1.0 + mod_ref[1:2, :]) + mod_ref[0:1, :]).astype(bf16)
    gg = _dot(h_ref[...], wgg_ref[...])
    oc = (on_ref[...] * (gg * jax.nn.sigmoid(gg))).astype(bf16)
    branches = ((oa_ref[...], wba_ref), (ob_ref[...], wbb_ref), (oc, wbc_ref))
    merged = None
    for n, (o, wb_ref) in enumerate(branches):
        gate = jax.nn.sigmoid(_dot(h_ref[...], wgate_ref[:, n * d:(n + 1) * d]))
        term = gate * _dot(o, wb_ref[...])
        merged = term if merged is None else merged + term
    y = _dot(merged.astype(bf16), wo_ref[...])
    r = DEEPNORM_ALPHA * x + (1.0 + mod_ref[2:3, :]) * y
    o_ref[...] = _layer_norm(r, lng_ref[...], lnb_ref[...])


def _merge(x, mod, oa, ob, on, wgate, wgg, wba, wbb, wbc, wo, lng, lnb):
    bsz, s, d = x.shape
    tm = MERGE_TM
    row = lambda w: pl.BlockSpec((None, tm, w), lambda b, i: (b, i, 0))
    return pl.pallas_call(
        _merge_kernel,
        grid=(bsz, s // tm),
        in_specs=[row(d), pl.BlockSpec((None, 3, d), lambda b, i: (b, 0, 0)),
                  row(ATT_W), row(ATT_W), row(GDN_W),
                  _const_spec(wgate.shape), _const_spec(wgg.shape), _const_spec(wba.shape),
                  _const_spec(wbb.shape), _const_spec(wbc.shape), _const_spec(wo.shape),
                  _const_spec((1, d)), _const_spec((1, d))],
        out_specs=row(d),
        out_shape=jax.ShapeDtypeStruct(x.shape, f32),
        scratch_shapes=[pltpu.VMEM((tm, d), bf16)],
        compiler_params=_params(("parallel", "parallel")),
        name="mixer_merge",
    )(x, mod, oa, ob, on, wgate, wgg, wba, wbb, wbc, wo, lng, lnb)


def _mixer_weights(w_in, b_forget, a_log, dt_bias):
    d = w_in.shape[0]
    sizes = (ATT_W, ATT_W, ATT_W, HEADS, ATT_W, ATT_W, ATT_W, 3 * GDN_W, HEADS, HEADS, GDN_W,
             3 * D_MODEL)
    offs = [0]
    for n in sizes:
        offs.append(offs[-1] + n)
    piece = lambda n: w_in[:, offs[n]:offs[n + 1]]
    wa = jnp.concatenate([piece(0), piece(1), piece(2)], axis=1).astype(bf16)
    wb = jnp.concatenate([piece(4), piece(5), piece(6)], axis=1).astype(bf16)
    wc = piece(7).astype(bf16)
    ws = jnp.concatenate([piece(3), piece(8), piece(9), piece(9),
                          jnp.zeros((d, SMALL_W - 4 * HEADS), w_in.dtype)], axis=1).astype(bf16)
    wgg = piece(10).astype(bf16)
    wgate = piece(11).astype(bf16)
    pad = lambda vec, at: jnp.zeros((SMALL_W,), f32).at[at:at + HEADS].set(vec.astype(f32))
    pv = jnp.stack([pad(b_forget, 0),
                    pad(a_log, 2 * HEADS) + pad(a_log, 3 * HEADS),
                    pad(dt_bias, 2 * HEADS) + pad(dt_bias, 3 * HEADS)]
                   + [jnp.zeros((SMALL_W,), f32)] * 5)
    return wa, wb, wc, ws, wgg, wgate, pv


def kernel(x, c, w_ada, b_ada, ln_g, ln_b, ffn_w_up, ffn_w_down, w_in, b_forget, conv_w,
           gdn_a_log, gdn_dt_bias, gdn_norm_g, w_branch, w_o):
    bsz = x.shape[0]
    mod = _ada(c, w_ada, b_ada).reshape(DEPTH, bsz, N_SUB, 3, D_MODEL)
    wup = ffn_w_up.astype(bf16)
    wdown = ffn_w_down.astype(bf16)
    for l in range(DEPTH):
        lng = lambda n: ln_g[l, n].reshape(1, D_MODEL)
        lnb = lambda n: ln_b[l, n].reshape(1, D_MODEL)
        x = _ffn(x, mod[l, :, 0], wup[l, 0], wdown[l, 0], lng(0), lnb(0))

        wa, wb, wc, ws, wgg, wgate, pv = _mixer_weights(w_in[l], b_forget[l], gdn_a_log[l],
                                                        gdn_dt_bias[l])
        m = mod[l, :, 1]
        fq, fkt, fv, sq, skt, sv, gq, gk, gv, sm = _proj(x, m, wa, wb, wc, ws, conv_w[l])
        p, pt = _prefix(sm, pv)
        oa = _fox(fq, fkt, fv, p, pt)
        ob = _sb(sq, skt, sv)
        on = _gdn(gq, gk, gv, p, pt, gdn_norm_g[l].reshape(1, GDN_HEAD))
        wbr = w_branch[l].astype(bf16)
        x = _merge(x, m, oa, ob, on, wgate, wgg, wbr[:ATT_W], wbr[ATT_W:2 * ATT_W],
                   wbr[2 * ATT_W:], w_o[l].astype(bf16), lng(1), lnb(1))

        x = _ffn(x, mod[l, :, 2], wup[l, 1], wdown[l, 1], lng(2), lnb(2))
    return x
```

```python
import functools

import jax
import jax.numpy as jnp
from jax import lax
from jax.experimental import pallas as pl
from jax.experimental.pallas import tpu as pltpu

f32 = jnp.float32
bf16 = jnp.bfloat16

D_MODEL = 1024
DEPTH = 2
N_SUB = 3
FFN_HIDDEN = 2816
HEADS = 4
ATT_W = 256
ATT_HEAD = 64
GDN_W = 512
GDN_HEAD = 128
CHUNK = 64
CONV_WIDTH = 4
SMALL_W = 128
DEEPNORM_ALPHA = (2.0 * DEPTH) ** 0.25
LN_EPS = 1e-5
RMS_EPS = 1e-6
LOG2E = 1.4426950408889634

FFN_TM = 512
FFN_HC = 256
PROJ_TM = 512
MERGE_TM = 512
LANES = 128
ATT_TQ = 256
ATT_TK = 256
FOX_TQ = 512
FOX_TK = 512
PREFIX_BLK = 256
GDN_SB = 128
GDN_TS = 1024
GDN_LOCAL_BLOCKS = 4
CONV_HALO = 8
NEG_BIG = -1e30
EXP_UNDERFLOW = -104.0
VMEM_LIMIT = 56 * 1024 * 1024


def _dot(a, b):
    return jnp.dot(a, b, preferred_element_type=f32)


def _dot_nt(a, b):
    return lax.dot_general(a, b, (((1,), (1,)), ((), ())), preferred_element_type=f32)


def _dot_tn(a, b):
    return lax.dot_general(a, b, (((0,), (0,)), ((), ())), preferred_element_type=f32)


def _split2(x):
    hi = x.astype(bf16)
    lo = (x - hi.astype(f32)).astype(bf16)
    return hi, lo


def _split3(x):
    hi = x.astype(bf16)
    r = x - hi.astype(f32)
    mid = r.astype(bf16)
    lo = (r - mid.astype(f32)).astype(bf16)
    return hi, mid, lo


def _dot3(a, b):
    ah, al = _split2(a)
    bh, bl = _split2(b)
    return _dot(ah, bh) + (_dot(ah, bl) + _dot(al, bh))


def _layer_norm(r, g, b):
    mu = jnp.mean(r, axis=-1, keepdims=True)
    d = r - mu
    var = jnp.mean(d * d, axis=-1, keepdims=True)
    return d * lax.rsqrt(var + LN_EPS) * g + b


def _softplus_neg_abs(x):
    return jnp.log1p(jnp.exp(-jnp.abs(x)))


def _const_spec(shape):
    nd = len(shape)
    return pl.BlockSpec(shape, lambda *_: (0,) * nd, pipeline_mode=pl.Buffered(1))


def _params(sem):
    return pltpu.CompilerParams(dimension_semantics=sem, vmem_limit_bytes=VMEM_LIMIT)


def _ada_kernel(c_ref, w_ref, b_ref, o_ref):
    c = c_ref[...]
    sc = c * jax.nn.sigmoid(c)
    o_ref[...] = _dot3(sc, w_ref[...]) + b_ref[...]


def _ada(c, w_ada, b_ada):
    depth, d, n = w_ada.shape
    bsz = c.shape[0]
    tn = 1024
    return pl.pallas_call(
        _ada_kernel,
        grid=(depth, n // tn),
        in_specs=[pl.BlockSpec((bsz, d), lambda l, j: (0, 0)),
                  pl.BlockSpec((None, d, tn), lambda l, j: (l, 0, j)),
                  pl.BlockSpec((None, 1, tn), lambda l, j: (l, 0, j))],
        out_specs=pl.BlockSpec((None, bsz, tn), lambda l, j: (l, 0, j)),
        out_shape=jax.ShapeDtypeStruct((depth, bsz, n), f32),
        compiler_params=_params(("parallel", "parallel")),
        name="adaln",
    )(c, w_ada, b_ada.reshape(depth, 1, n))


def _ffn_kernel(x_ref, mod_ref, wup_ref, wd_ref, lng_ref, lnb_ref, o_ref, h_ref, acc_ref):
    x = x_ref[...]
    h_ref[...] = (x * (1.0 + mod_ref[1:2, :]) + mod_ref[0:1, :]).astype(bf16)
    for j in range(FFN_HIDDEN // FFN_HC):
        lo = j * FFN_HC
        h = h_ref[...]
        g = _dot(h, wup_ref[:, lo:lo + FFN_HC])
        u = _dot(h, wup_ref[:, FFN_HIDDEN + lo:FFN_HIDDEN + lo + FFN_HC])
        a = (g * jax.nn.sigmoid(g) * u).astype(bf16)
        y = _dot(a, wd_ref[lo:lo + FFN_HC, :])
        if j == 0:
            acc_ref[...] = y
        else:
            acc_ref[...] += y
    r = DEEPNORM_ALPHA * x + (0.5 * (1.0 + mod_ref[2:3, :])) * acc_ref[...]
    o_ref[...] = _layer_norm(r, lng_ref[...], lnb_ref[...])


def _ffn(x, mod, wup, wd, lng, lnb):
    bsz, s, d = x.shape
    tm = FFN_TM
    return pl.pallas_call(
        _ffn_kernel,
        grid=(bsz, s // tm),
        in_specs=[pl.BlockSpec((None, tm, d), lambda b, i: (b, i, 0)),
                  pl.BlockSpec((None, 3, d), lambda b, i: (b, 0, 0)),
                  _const_spec(wup.shape), _const_spec(wd.shape),
                  _const_spec((1, d)), _const_spec((1, d))],
        out_specs=pl.BlockSpec((None, tm, d), lambda b, i: (b, i, 0)),
        out_shape=jax.ShapeDtypeStruct(x.shape, f32),
        scratch_shapes=[pltpu.VMEM((tm, d), bf16), pltpu.VMEM((tm, d), f32)],
        compiler_params=_params(("parallel", "parallel")),
        name="ffn",
    )(x, mod, wup, wd, lng, lnb)


def _proj_kernel(x_ref, mod_ref, wa_ref, wb_ref, wc_ref, ws_ref, cw_ref,
                 fq_ref, fkt_ref, fv_ref, sq_ref, skt_ref, sv_ref,
                 gq_ref, gk_ref, gv_ref, sm_ref, h_ref, cbuf_ref):
    tm = x_ref.shape[0]
    h_ref[...] = (x_ref[...] * (1.0 + mod_ref[1:2, :]) + mod_ref[0:1, :]).astype(bf16)

    def attn_group(w_ref, q_ref, kt_ref, v_ref, q_scale):
        r = _dot(h_ref[...], w_ref[...])
        q_ref[...] = (r[:, :ATT_W] * q_scale).astype(bf16)
        kt_ref[...] = r[:, ATT_W:2 * ATT_W].T.astype(bf16)
        v_ref[...] = r[:, 2 * ATT_W:].astype(bf16)

    @pl.when(pl.program_id(1) == 0)
    def _():
        cbuf_ref[0:CONV_HALO, :] = jnp.zeros((CONV_HALO, cbuf_ref.shape[1]), f32)

    def conv_act(part):
        cols = slice(part * GDN_W, (part + 1) * GDN_W)
        cbuf_ref[CONV_HALO:, cols] = _dot(h_ref[...], wc_ref[:, cols])
        conv = cbuf_ref[CONV_HALO:, cols] * cw_ref[CONV_WIDTH - 1:CONV_WIDTH, cols]
        for tap in range(CONV_WIDTH - 1):
            back = CONV_WIDTH - 1 - tap
            conv = conv + (cbuf_ref[CONV_HALO - back:CONV_HALO - back + tm, cols]
                           * cw_ref[tap:tap + 1, cols])
        cbuf_ref[0:CONV_HALO, cols] = cbuf_ref[tm:tm + CONV_HALO, cols]
        return conv * jax.nn.sigmoid(conv)

    def l2norm_heads(act, o_ref, scale):
        for hd in range(HEADS):
            lanes = slice(hd * GDN_HEAD, (hd + 1) * GDN_HEAD)
            t = act[:, lanes]
            o_ref[:, lanes] = t * lax.rsqrt(jnp.sum(t * t, -1, keepdims=True) + RMS_EPS) * scale

    l2norm_heads(conv_act(0), gq_ref, GDN_HEAD ** -0.5)
    attn_group(wa_ref, fq_ref, fkt_ref, fv_ref, LOG2E * ATT_HEAD ** -0.5)
    l2norm_heads(conv_act(1), gk_ref, 1.0)
    attn_group(wb_ref, sq_ref, skt_ref, sv_ref, ATT_HEAD ** -0.5)
    gv_ref[...] = conv_act(2)
    sm_ref[...] = _dot(h_ref[...], ws_ref[...])


def _proj(x, mod, wa, wb, wc, ws, cw):
    bsz, s, d = x.shape
    tm = PROJ_TM
    row = lambda w: pl.BlockSpec((None, tm, w), lambda b, i: (b, i, 0))
    col = pl.BlockSpec((None, ATT_W, tm), lambda b, i: (b, 0, i))
    sd = jax.ShapeDtypeStruct
    return pl.pallas_call(
        _proj_kernel,
        grid=(bsz, s // tm),
        in_specs=[row(d), pl.BlockSpec((None, 3, d), lambda b, i: (b, 0, 0)),
                  _const_spec(wa.shape), _const_spec(wb.shape), _const_spec(wc.shape),
                  _const_spec(ws.shape), _const_spec(cw.shape)],
        out_specs=[row(ATT_W), col, row(ATT_W), row(ATT_W), col, row(ATT_W),
                   row(GDN_W), row(GDN_W), row(GDN_W), row(SMALL_W)],
        out_shape=[sd((bsz, s, ATT_W), bf16), sd((bsz, ATT_W, s), bf16), sd((bsz, s, ATT_W), bf16),
                   sd((bsz, s, ATT_W), bf16), sd((bsz, ATT_W, s), bf16), sd((bsz, s, ATT_W), bf16),
                   sd((bsz, s, GDN_W), f32), sd((bsz, s, GDN_W), f32), sd((bsz, s, GDN_W), f32),
                   sd((bsz, s, SMALL_W), f32)],
        scratch_shapes=[pltpu.VMEM((tm, d), bf16), pltpu.VMEM((tm + CONV_HALO, 3 * GDN_W), f32)],
        compiler_params=_params(("parallel", "arbitrary")),
        name="mixer_in_proj",
    )(x, mod, wa, wb, wc, ws, cw)


def _prefix_kernel(sm_ref, pv_ref, p_ref, pt_ref):
    s = sm_ref.shape[0]
    blk = PREFIX_BLK
    ri = lax.broadcasted_iota(jnp.int32, (blk, blk), 0)
    ci = lax.broadcasted_iota(jnp.int32, (blk, blk), 1)
    low = ri >= ci
    same = (ri // CHUNK) == (ci // CHUNK)
    tri_full = jnp.where(low, 1.0, 0.0).astype(bf16)
    tri_chunk = jnp.where(low & same, 1.0, 0.0).astype(bf16)
    ones_chunk = jnp.where(same, 1.0, 0.0).astype(bf16)
    lane = lax.broadcasted_iota(jnp.int32, (blk, SMALL_W), 1)
    b_forget = pv_ref[0:1, :]
    neg_a = -jnp.exp(pv_ref[1:2, :])
    dt_bias = pv_ref[2:3, :]
    carry = jnp.zeros((1, SMALL_W), f32)
    for i in range(s // blk):
        x = sm_ref[i * blk:(i + 1) * blk, :]
        xf = x + b_forget
        log_f = jnp.minimum(xf, 0.0) - _softplus_neg_abs(xf)
        beta = jax.nn.sigmoid(x)
        xg = x + dt_bias
        g = neg_a * (jnp.maximum(xg, 0.0) + _softplus_neg_abs(xg))
        val = jnp.where(lane < 4, log_f, jnp.where(lane < 8, beta, jnp.where(lane < 16, g, 0.0)))
        parts = _split3(val)
        cum_full = sum(_dot(tri_full, p) for p in parts) + carry
        cum_chunk = sum(_dot(tri_chunk, p) for p in parts)
        tot_chunk = sum(_dot(ones_chunk, p) for p in parts)
        carry = cum_full[blk - 1:blk, :]
        out = jnp.where(lane < 4, cum_full,
                        jnp.where(lane < 8, val, jnp.where(lane < 12, cum_chunk, tot_chunk)))
        p_ref[i * blk:(i + 1) * blk, :] = out
        pt_ref[:, i * blk:(i + 1) * blk] = out.T[0:16, :]


def _prefix(sm, pv):
    bsz, s, _ = sm.shape
    return pl.pallas_call(
        _prefix_kernel,
        grid=(bsz,),
        in_specs=[pl.BlockSpec((None, s, SMALL_W), lambda b: (b, 0, 0)), _const_spec(pv.shape)],
        out_specs=[pl.BlockSpec((None, s, SMALL_W), lambda b: (b, 0, 0)),
                   pl.BlockSpec((None, 16, s), lambda b: (b, 0, 0))],
        out_shape=[jax.ShapeDtypeStruct((bsz, s, SMALL_W), f32),
                   jax.ShapeDtypeStruct((bsz, 16, s), f32)],
        compiler_params=_params(("parallel",)),
        name="gate_prefix",
    )(sm, pv)


def _head_masks(shape, axis):
    lane = lax.broadcasted_iota(jnp.int32, shape, axis)
    return [(lane >= hd * ATT_HEAD) & (lane < (hd + 1) * ATT_HEAD) for hd in range(HEADS)]


def _by_head(masks, cols):
    out = cols[HEADS - 1]
    for hd in range(HEADS - 2, -1, -1):
        out = jnp.where(masks[hd], cols[hd], out)
    return out


def _fox_kernel(q_ref, kt_ref, v_ref, p_ref, pt_ref, o_ref,
                vm_ref, s_ref, cq_ref, m_ref, l_ref, acc_ref):
    tq, tk, lw = FOX_TQ, FOX_TK, LANES
    hs = range(HEADS)
    groups = range(tk // lw)
    i = pl.program_id(1)
    q = q_ref[...]
    qmask = _head_masks((tq, ATT_W), 1)
    zero = jnp.zeros((), bf16)
    qm = [jnp.where(qmask[hd], q, zero) for hd in hs]

    @pl.when(i == 0)
    def _():
        vmask = _head_masks(v_ref.shape, 1)
        v = v_ref[...]
        for hd in hs:
            vm_ref[hd] = jnp.where(vmask[hd], v, zero)

    delta = (lax.broadcasted_iota(jnp.int32, (tq, lw), 1)
             - lax.broadcasted_iota(jnp.int32, (tq, lw), 0))
    n_full = (i * tq) // tk

    def score_block(j, _, masked=False):
        off = pl.multiple_of(j * tk, tk)
        kt = kt_ref[:, pl.ds(off, tk)]
        for hd in hs:
            s = _dot(qm[hd], kt)
            cq = cq_ref[hd]
            ck = pt_ref[hd:hd + 1, pl.ds(off, tk)] * LOG2E
            mx = m_ref[hd]
            for g in groups:
                sg = s[:, g * lw:(g + 1) * lw] + cq - ck[:, g * lw:(g + 1) * lw]
                if masked:
                    sg = jnp.where(delta <= i * tq - off - g * lw, sg, NEG_BIG)
                s_ref[hd, :, pl.ds(off + g * lw, lw)] = sg
                mx = jnp.maximum(mx, sg)
            m_ref[hd] = mx
        return 0

    m_ref[...] = jnp.full(m_ref.shape, NEG_BIG, f32)
    for hd in hs:
        cq_ref[hd] = jnp.broadcast_to(p_ref[:, hd:hd + 1], (tq, lw)) * LOG2E
    lax.fori_loop(0, n_full, score_block, 0)
    score_block(n_full, 0, masked=True)
    for hd in hs:
        m_ref[hd] = jnp.broadcast_to(jnp.max(m_ref[hd], axis=-1, keepdims=True), (tq, lw))

    def sum_block(j, _):
        off = pl.multiple_of(j * tk, tk)
        pv = None
        for hd in hs:
            mx = m_ref[hd]
            p = jnp.exp2(s_ref[hd, :, pl.ds(off, tk)] - jnp.concatenate([mx] * len(groups), axis=1))
            l = l_ref[hd]
            for g in groups:
                l = l + p[:, g * lw:(g + 1) * lw]
            l_ref[hd] = l
            c = _dot(p.astype(bf16), vm_ref[hd, pl.ds(off, tk), :])
            pv = c if pv is None else pv + c
        acc_ref[...] += pv
        return 0

    l_ref[...] = jnp.zeros(l_ref.shape, f32)
    acc_ref[...] = jnp.zeros(acc_ref.shape, f32)
    lax.fori_loop(0, n_full + 1, sum_block, 0)
    l = [jnp.sum(l_ref[hd], axis=-1, keepdims=True) for hd in hs]
    o_ref[...] = (acc_ref[...] / _by_head(qmask, l)).astype(bf16)


def _fox(q, kt, v, p, pt):
    bsz, s, _ = q.shape
    tq = FOX_TQ
    return pl.pallas_call(
        _fox_kernel,
        grid=(bsz, s // tq),
        in_specs=[pl.BlockSpec((None, tq, ATT_W), lambda b, i: (b, i, 0)),
                  pl.BlockSpec((None, ATT_W, s), lambda b, i: (b, 0, 0)),
                  pl.BlockSpec((None, s, ATT_W), lambda b, i: (b, 0, 0)),
                  pl.BlockSpec((None, tq, SMALL_W), lambda b, i: (b, i, 0)),
                  pl.BlockSpec((None, 16, s), lambda b, i: (b, 0, 0))],
        out_specs=pl.BlockSpec((None, tq, ATT_W), lambda b, i: (b, i, 0)),
        out_shape=jax.ShapeDtypeStruct((bsz, s, ATT_W), bf16),
        scratch_shapes=[pltpu.VMEM((HEADS, s, ATT_W), bf16),
                        pltpu.VMEM((HEADS, tq, s), f32),
                        pltpu.VMEM((HEADS, tq, LANES), f32),
                        pltpu.VMEM((HEADS, tq, LANES), f32),
                        pltpu.VMEM((HEADS, tq, LANES), f32),
                        pltpu.VMEM((tq, ATT_W), f32)],
        compiler_params=_params(("parallel", "arbitrary")),
        name="forgetting_attention",
    )(q, kt, v, p, pt)


def _sb_kernel(q_ref, kt_ref, v_ref, o_ref):
    tq, tk = ATT_TQ, ATT_TK
    i = pl.program_id(1)
    q = q_ref[...]
    qmask = _head_masks((tq, ATT_W), 1)
    vmask = _head_masks((tk, ATT_W), 1)
    zero = jnp.zeros((), bf16)
    qm = [jnp.where(qmask[hd], q, zero) for hd in range(HEADS)]
    rows = lax.broadcasted_iota(jnp.int32, (tq, tk), 0)
    cols = lax.broadcasted_iota(jnp.int32, (tq, tk), 1)
    strict = cols < rows
    kr = lax.broadcasted_iota(jnp.int32, (tk, tk), 0)
    kc = lax.broadcasted_iota(jnp.int32, (tk, tk), 1)
    later = jnp.where(kr > kc, 1.0, 0.0).astype(bf16)

    def block(j, rem, acc, diag):
        off = pl.multiple_of(j * tk, tk)
        kt = kt_ref[:, pl.ds(off, tk)]
        v = v_ref[pl.ds(off, tk), :]
        hs = range(HEADS)
        z = [_dot(qm[hd], kt) for hd in hs]
        log_beta = [jnp.minimum(z[hd], 0.0) - jnp.log(1.0 + jnp.exp(-jnp.abs(z[hd])))
                    for hd in hs]
        log_rem = [log_beta[hd] - z[hd] for hd in hs]
        if diag:
            log_rem = [jnp.where(strict, log_rem[hd], 0.0) for hd in hs]
        parts = [_split2(log_rem[hd]) for hd in hs]
        after = [_dot(parts[hd][0], later) + _dot(parts[hd][1], later) + rem[hd] for hd in hs]
        w = [jnp.exp(log_beta[hd] + after[hd]) for hd in hs]
        if diag:
            w = [jnp.where(strict, w[hd], 0.0) for hd in hs]
        rem_new = [rem[hd] + jnp.sum(log_rem[hd], axis=-1, keepdims=True) for hd in hs]
        for hd in hs:
            acc = acc + _dot(w[hd].astype(bf16), jnp.where(vmask[hd], v, zero))
        return tuple(rem_new), acc

    rem, acc = block(i, tuple(jnp.zeros((tq, 1), f32) for _ in range(HEADS)),
                     jnp.zeros((tq, ATT_W), f32), diag=True)

    def live(rem):
        top = rem[0]
        for hd in range(1, HEADS):
            top = jnp.maximum(top, rem[hd])
        return jnp.max(top) > EXP_UNDERFLOW

    def cond(state):
        return jnp.logical_and(state[0] >= 0, state[1])

    def body(state):
        j, _, rem, acc = state
        rem, acc = block(j, rem, acc, diag=False)
        return j - 1, live(rem), rem, acc

    _, _, _, acc = lax.while_loop(cond, body, (i - 1, live(rem), rem, acc))
    o_ref[...] = acc.astype(bf16)


def _sb(q, kt, v):
    bsz, s, _ = q.shape
    tq = ATT_TQ
    return pl.pallas_call(
        _sb_kernel,
        grid=(bsz, s // tq),
        in_specs=[pl.BlockSpec((None, tq, ATT_W), lambda b, i: (b, i, 0)),
                  pl.BlockSpec((None, ATT_W, s), lambda b, i: (b, 0, 0)),
                  pl.BlockSpec((None, s, ATT_W), lambda b, i: (b, 0, 0))],
        out_specs=pl.BlockSpec((None, tq, ATT_W), lambda b, i: (b, i, 0)),
        out_shape=jax.ShapeDtypeStruct((bsz, s, ATT_W), bf16),
        compiler_params=_params(("parallel", "parallel")),
        name="stick_breaking_attention",
    )(q, kt, v)


def _gdn_kernel(q_ref, k_ref, v_ref, p_ref, pt_ref, ng_ref, o_ref,
                qe_ref, ol_ref, g_ref, b_ref, dec_ref, state_ref):
    ts = q_ref.shape[0]
    sb = GDN_SB
    hs = range(HEADS)
    lanes = [slice(hd * GDN_HEAD, (hd + 1) * GDN_HEAD) for hd in hs]
    ri = lax.broadcasted_iota(jnp.int32, (sb, sb), 0)
    ci = lax.broadcasted_iota(jnp.int32, (sb, sb), 1)
    same = (ri // CHUNK) == (ci // CHUNK)
    incl = same & (ri >= ci)
    strict = same & (ri > ci)
    cast = lambda t: t.astype(bf16)

    @pl.when(pl.program_id(1) == 0)
    def _():
        state_ref[...] = jnp.zeros(state_ref.shape, f32)

    def local(i, _):
        units = [(blk, h) for blk in range(GDN_LOCAL_BLOCKS) for h in hs]
        us = range(len(units))
        bi = [i * GDN_LOCAL_BLOCKS + blk for blk, _ in units]
        rows = [pl.ds(pl.multiple_of(bi[u] * sb, sb), sb) for u in us]
        hd = [h for _, h in units]
        k = [k_ref[rows[u], lanes[hd[u]]] for u in us]
        beta = [p_ref[rows[u], 4 + hd[u]:5 + hd[u]] for u in us]
        gc = [p_ref[rows[u], 8 + hd[u]:9 + hd[u]] for u in us]
        gl = [p_ref[rows[u], 12 + hd[u]:13 + hd[u]] for u in us]
        decay = [jnp.where(incl, jnp.exp(jnp.where(
            incl, gc[u] - pt_ref[8 + hd[u]:9 + hd[u], rows[u]], 0.0)), 0.0) for u in us]
        kb = [k[u] * beta[u] for u in us]
        kbf = [cast(k[u]) for u in us]
        a = [jnp.where(strict, _dot_nt(cast(kb[u]), kbf[u]) * decay[u], 0.0) for u in us]
        ab = [cast(a[u]) for u in us]
        pw = [_dot(ab[u], ab[u]) for u in us]
        n = [-a[u] for u in us]
        for level in range(5):
            pb = [cast(pw[u]) for u in us]
            n = [n[u] + pw[u] + _dot(cast(n[u]), pb[u]) for u in us]
            if level < 4:
                pw = [_dot(pb[u], pb[u]) for u in us]
        e_gc = [jnp.exp(gc[u]) for u in us]
        rhs = [jnp.concatenate([v_ref[rows[u], lanes[hd[u]]] * beta[u], kb[u] * e_gc[u]], axis=1)
               for u in us]
        sol = [rhs[u] + _dot(cast(n[u]), cast(rhs[u])) for u in us]
        solb = [cast(sol[u]) for u in us]
        q = [q_ref[rows[u], lanes[hd[u]]] for u in us]
        intra = [cast(jnp.where(incl, _dot_nt(cast(q[u]), kbf[u]) * decay[u], 0.0)) for u in us]
        prod = [_dot(intra[u], solb[u]) for u in us]
        kd = [cast(k[u] * jnp.exp(gl[u] - gc[u])) for u in us]
        for u in us:
            h = hd[u]
            ol_ref[rows[u], lanes[h]] = prod[u][:, :GDN_HEAD]
            qe_ref[rows[u], lanes[h]] = cast(q[u] * e_gc[u] - prod[u][:, GDN_HEAD:])
            glb = jnp.exp(jnp.broadcast_to(gl[u], (sb, GDN_HEAD)))
            for c in range(2):
                crow = slice(c * CHUNK, (c + 1) * CHUNK)
                bg = _dot_tn(kd[u][crow], solb[u][crow])
                chunk = 2 * bi[u] + c
                grow = pl.ds(pl.multiple_of(chunk * GDN_HEAD, GDN_HEAD), GDN_HEAD)
                b_ref[h, grow, :] = bg[:, :GDN_HEAD]
                g_ref[h, grow, :] = cast(bg[:, GDN_HEAD:])
                dec_ref[h, pl.ds(chunk, 1), :] = glb[c * CHUNK:c * CHUNK + 1, :]
        return 0

    lax.fori_loop(0, ts // (sb * GDN_LOCAL_BLOCKS), local, 0)

    def scan(c, _):
        crow = pl.ds(pl.multiple_of(c * CHUNK, CHUNK), CHUNK)
        grow = pl.ds(pl.multiple_of(c * GDN_HEAD, GDN_HEAD), GDN_HEAD)
        state = [state_ref[h] for h in hs]
        sbf = [cast(state[h]) for h in hs]
        gs = [_dot(g_ref[h, grow, :], sbf[h]) for h in hs]
        for h in hs:
            state_ref[h] = state[h] * dec_ref[h, pl.ds(c, 1), :] - gs[h] + b_ref[h, grow, :]
        o = [_dot(qe_ref[crow, lanes[h]], sbf[h]) + ol_ref[crow, lanes[h]] for h in hs]
        for h in hs:
            o_ref[crow, lanes[h]] = (o[h] * lax.rsqrt(jnp.mean(o[h] * o[h], axis=-1, keepdims=True)
                                                      + RMS_EPS) * ng_ref[...])
        return 0

    lax.fori_loop(0, ts // CHUNK, scan, 0)


def _gdn(q, k, v, p, pt, ng):
    bsz, s, _ = q.shape
    ts = min(GDN_TS, s)
    n_chunks = ts // CHUNK
    row = lambda w: pl.BlockSpec((None, ts, w), lambda b, i: (b, i, 0))
    return pl.pallas_call(
        _gdn_kernel,
        grid=(bsz, s // ts),
        in_specs=[row(GDN_W), row(GDN_W), row(GDN_W), row(SMALL_W),
                  pl.BlockSpec((None, 16, ts), lambda b, i: (b, 0, i)), _const_spec(ng.shape)],
        out_specs=row(GDN_W),
        out_shape=jax.ShapeDtypeStruct((bsz, s, GDN_W), f32),
        scratch_shapes=[pltpu.VMEM((ts, GDN_W), bf16), pltpu.VMEM((ts, GDN_W), f32),
                        pltpu.VMEM((HEADS, n_chunks * GDN_HEAD, GDN_HEAD), bf16),
                        pltpu.VMEM((HEADS, n_chunks * GDN_HEAD, GDN_HEAD), f32),
                        pltpu.VMEM((HEADS, n_chunks, GDN_HEAD), f32),
                        pltpu.VMEM((HEADS, GDN_HEAD, GDN_HEAD), f32)],
        compiler_params=_params(("parallel", "arbitrary")),
        name="gated_delta_rule",
    )(q, k, v, p, pt, ng)


def _merge_kernel(x_ref, mod_ref, oa_ref, ob_ref, on_ref, wgate_ref, wgg_ref,
                  wba_ref, wbb_ref, wbc_ref, wo_ref, lng_ref, lnb_ref, o_ref, h_ref):
    x = x_ref[...]
    d = x.shape[1]
    h_ref[...] = (x * (1.0 + mod_ref[1:2, :]) + mod_ref[0:1, :]).astype(bf16)
    gg = _dot(h_ref[...], wgg_ref[...])
    oc = (on_ref[...] * (gg * jax.nn.sigmoid(gg))).astype(bf16)
    branches = ((oa_ref[...], wba_ref), (ob_ref[...], wbb_ref), (oc, wbc_ref))
    merged = None
    for n, (o, wb_ref) in enumerate(branches):
        gate = jax.nn.sigmoid(_dot(h_ref[...], wgate_ref[:, n * d:(n + 1) * d]))
        term = gate * _dot(o, wb_ref[...])
        merged = term if merged is None else merged + term
    y = _dot(merged.astype(bf16), wo_ref[...])
    r = DEEPNORM_ALPHA * x + (1.0 + mod_ref[2:3, :]) * y
    o_ref[...] = _layer_norm(r, lng_ref[...], lnb_ref[...])


def _merge(x, mod, oa, ob, on, wgate, wgg, wba, wbb, wbc, wo, lng, lnb):
    bsz, s, d = x.shape
    tm = MERGE_TM
    row = lambda w: pl.BlockSpec((None, tm, w), lambda b, i: (b, i, 0))
    return pl.pallas_call(
        _merge_kernel,
        grid=(bsz, s // tm),
        in_specs=[row(d), pl.BlockSpec((None, 3, d), lambda b, i: (b, 0, 0)),
                  row(ATT_W), row(ATT_W), row(GDN_W),
                  _const_spec(wgate.shape), _const_spec(wgg.shape), _const_spec(wba.shape),
                  _const_spec(wbb.shape), _const_spec(wbc.shape), _const_spec(wo.shape),
                  _const_spec((1, d)), _const_spec((1, d))],
        out_specs=row(d),
        out_shape=jax.ShapeDtypeStruct(x.shape, f32),
        scratch_shapes=[pltpu.VMEM((tm, d), bf16)],
        compiler_params=_params(("parallel", "parallel")),
        name="mixer_merge",
    )(x, mod, oa, ob, on, wgate, wgg, wba, wbb, wbc, wo, lng, lnb)


def _mixer_weights(w_in, b_forget, a_log, dt_bias):
    d = w_in.shape[0]
    sizes = (ATT_W, ATT_W, ATT_W, HEADS, ATT_W, ATT_W, ATT_W, 3 * GDN_W, HEADS, HEADS, GDN_W,
             3 * D_MODEL)
    offs = [0]
    for n in sizes:
        offs.append(offs[-1] + n)
    piece = lambda n: w_in[:, offs[n]:offs[n + 1]]
    wa = jnp.concatenate([piece(0), piece(1), piece(2)], axis=1).astype(bf16)
    wb = jnp.concatenate([piece(4), piece(5), piece(6)], axis=1).astype(bf16)
    wc = piece(7).astype(bf16)
    ws = jnp.concatenate([piece(3), piece(8), piece(9), piece(9),
                          jnp.zeros((d, SMALL_W - 4 * HEADS), w_in.dtype)], axis=1).astype(bf16)
    wgg = piece(10).astype(bf16)
    wgate = piece(11).astype(bf16)
    pad = lambda vec, at: jnp.zeros((SMALL_W,), f32).at[at:at + HEADS].set(vec.astype(f32))
    pv = jnp.stack([pad(b_forget, 0),
                    pad(a_log, 2 * HEADS) + pad(a_log, 3 * HEADS),
                    pad(dt_bias, 2 * HEADS) + pad(dt_bias, 3 * HEADS)]
                   + [jnp.zeros((SMALL_W,), f32)] * 5)
    return wa, wb, wc, ws, wgg, wgate, pv


def kernel(x, c, w_ada, b_ada, ln_g, ln_b, ffn_w_up, ffn_w_down, w_in, b_forget, conv_w,
           gdn_a_log, gdn_dt_bias, gdn_norm_g, w_branch, w_o):
    bsz = x.shape[0]
    mod = _ada(c, w_ada, b_ada).reshape(DEPTH, bsz, N_SUB, 3, D_MODEL)
    wup = ffn_w_up.astype(bf16)
    wdown = ffn_w_down.astype(bf16)
    for l in range(DEPTH):
        lng = lambda n: ln_g[l, n].reshape(1, D_MODEL)
        lnb = lambda n: ln_b[l, n].reshape(1, D_MODEL)
        x = _ffn(x, mod[l, :, 0], wup[l, 0], wdown[l, 0], lng(0), lnb(0))

        wa, wb, wc, ws, wgg, wgate, pv = _mixer_weights(w_in[l], b_forget[l], gdn_a_log[l],
                                                        gdn_dt_bias[l])
        m = mod[l, :, 1]
        fq, fkt, fv, sq, skt, sv, gq, gk, gv, sm = _proj(x, m, wa, wb, wc, ws, conv_w[l])
        p, pt = _prefix(sm, pv)
        oa = _fox(fq, fkt, fv, p, pt)
        ob = _sb(sq, skt, sv)
        on = _gdn(gq, gk, gv, p, pt, gdn_norm_g[l].reshape(1, GDN_HEAD))
        wbr = w_branch[l].astype(bf16)
        x = _merge(x, m, oa, ob, on, wgate, wgg, wbr[:ATT_W], wbr[ATT_W:2 * ATT_W],
                   wbr[2 * ATT_W:], w_o[l].astype(bf16), lng(1), lnb(1))

        x = _ffn(x, mod[l, :, 2], wup[l, 1], wdown[l, 1], lng(2), lnb(2))
    return x
```

```python
import functools

import jax
import jax.numpy as jnp
from jax import lax
from jax.experimental import pallas as pl
from jax.experimental.pallas import tpu as pltpu

f32 = jnp.float32
bf16 = jnp.bfloat16

D_MODEL = 1024
DEPTH = 2
N_SUB = 3
FFN_HIDDEN = 2816
HEADS = 4
ATT_W = 256
ATT_HEAD = 64
GDN_W = 512
GDN_HEAD = 128
CHUNK = 64
CONV_WIDTH = 4
SMALL_W = 128
DEEPNORM_ALPHA = (2.0 * DEPTH) ** 0.25
LN_EPS = 1e-5
RMS_EPS = 1e-6
LOG2E = 1.4426950408889634

FFN_TM = 512
FFN_HC = 256
PROJ_TM = 512
MERGE_TM = 512
LANES = 128
ATT_TQ = 256
ATT_TK = 256
FOX_TQ = 512
FOX_TK = 512
PREFIX_BLK = 256
GDN_SB = 128
GDN_TS = 1024
GDN_LOCAL_BLOCKS = 4
GDN_NORM_ROWS = 256
CONV_HALO = 8
NEG_BIG = -1e30
EXP2_UNDERFLOW = -151.0
VMEM_LIMIT = 56 * 1024 * 1024


def _dot(a, b):
    return jnp.dot(a, b, preferred_element_type=f32)


def _dot_nt(a, b):
    return lax.dot_general(a, b, (((1,), (1,)), ((), ())), preferred_element_type=f32)


def _dot_tn(a, b):
    return lax.dot_general(a, b, (((0,), (0,)), ((), ())), preferred_element_type=f32)


def _split2(x):
    hi = x.astype(bf16)
    lo = (x - hi.astype(f32)).astype(bf16)
    return hi, lo


def _split3(x):
    hi = x.astype(bf16)
    r = x - hi.astype(f32)
    mid = r.astype(bf16)
    lo = (r - mid.astype(f32)).astype(bf16)
    return hi, mid, lo


def _dot3(a, b):
    ah, al = _split2(a)
    bh, bl = _split2(b)
    return _dot(ah, bh) + (_dot(ah, bl) + _dot(al, bh))


def _layer_norm(r, g, b):
    mu = jnp.mean(r, axis=-1, keepdims=True)
    d = r - mu
    var = jnp.mean(d * d, axis=-1, keepdims=True)
    return d * lax.rsqrt(var + LN_EPS) * g + b


def _softplus_neg_abs(x):
    return jnp.log1p(jnp.exp(-jnp.abs(x)))


def _const_spec(shape):
    nd = len(shape)
    return pl.BlockSpec(shape, lambda *_: (0,) * nd, pipeline_mode=pl.Buffered(1))


def _params(sem):
    return pltpu.CompilerParams(dimension_semantics=sem, vmem_limit_bytes=VMEM_LIMIT)


def _ada_kernel(c_ref, w_ref, b_ref, o_ref):
    c = c_ref[...]
    sc = c * jax.nn.sigmoid(c)
    o_ref[...] = _dot3(sc, w_ref[...]) + b_ref[...]


def _ada(c, w_ada, b_ada):
    depth, d, n = w_ada.shape
    bsz = c.shape[0]
    tn = 1024
    return pl.pallas_call(
        _ada_kernel,
        grid=(depth, n // tn),
        in_specs=[pl.BlockSpec((bsz, d), lambda l, j: (0, 0)),
                  pl.BlockSpec((None, d, tn), lambda l, j: (l, 0, j)),
                  pl.BlockSpec((None, 1, tn), lambda l, j: (l, 0, j))],
        out_specs=pl.BlockSpec((None, bsz, tn), lambda l, j: (l, 0, j)),
        out_shape=jax.ShapeDtypeStruct((depth, bsz, n), f32),
        compiler_params=_params(("parallel", "parallel")),
        name="adaln",
    )(c, w_ada, b_ada.reshape(depth, 1, n))


def _ffn_kernel(x_ref, mod_ref, wup_ref, wd_ref, lng_ref, lnb_ref, o_ref, h_ref, acc_ref):
    x = x_ref[...]
    h_ref[...] = (x * (1.0 + mod_ref[1:2, :]) + mod_ref[0:1, :]).astype(bf16)
    for j in range(FFN_HIDDEN // FFN_HC):
        lo = j * FFN_HC
        h = h_ref[...]
        g = _dot(h, wup_ref[:, lo:lo + FFN_HC])
        u = _dot(h, wup_ref[:, FFN_HIDDEN + lo:FFN_HIDDEN + lo + FFN_HC])
        a = (g * jax.nn.sigmoid(g) * u).astype(bf16)
        y = _dot(a, wd_ref[lo:lo + FFN_HC, :])
        if j == 0:
            acc_ref[...] = y
        else:
            acc_ref[...] += y
    r = DEEPNORM_ALPHA * x + (0.5 * (1.0 + mod_ref[2:3, :])) * acc_ref[...]
    o_ref[...] = _layer_norm(r, lng_ref[...], lnb_ref[...])


def _ffn(x, mod, wup, wd, lng, lnb):
    bsz, s, d = x.shape
    tm = FFN_TM
    return pl.pallas_call(
        _ffn_kernel,
        grid=(bsz, s // tm),
        in_specs=[pl.BlockSpec((None, tm, d), lambda b, i: (b, i, 0)),
                  pl.BlockSpec((None, 3, d), lambda b, i: (b, 0, 0)),
                  _const_spec(wup.shape), _const_spec(wd.shape),
                  _const_spec((1, d)), _const_spec((1, d))],
        out_specs=pl.BlockSpec((None, tm, d), lambda b, i: (b, i, 0)),
        out_shape=jax.ShapeDtypeStruct(x.shape, f32),
        scratch_shapes=[pltpu.VMEM((tm, d), bf16), pltpu.VMEM((tm, d), f32)],
        compiler_params=_params(("parallel", "parallel")),
        name="ffn",
    )(x, mod, wup, wd, lng, lnb)


def _proj_kernel(x_ref, mod_ref, wa_ref, wb_ref, wc_ref, ws_ref, cw_ref,
                 fq_ref, fkt_ref, fv_ref, sq_ref, skt_ref, sv_ref,
                 gq_ref, gk_ref, gv_ref, sm_ref, h_ref, cbuf_ref):
    tm = x_ref.shape[0]
    h_ref[...] = (x_ref[...] * (1.0 + mod_ref[1:2, :]) + mod_ref[0:1, :]).astype(bf16)

    def attn_group(w_ref, q_ref, kt_ref, v_ref, q_scale):
        r = _dot(h_ref[...], w_ref[...])
        q_ref[...] = (r[:, :ATT_W] * q_scale).astype(bf16)
        kt_ref[...] = r[:, ATT_W:2 * ATT_W].T.astype(bf16)
        v_ref[...] = r[:, 2 * ATT_W:].astype(bf16)

    @pl.when(pl.program_id(1) == 0)
    def _():
        cbuf_ref[0:CONV_HALO, :] = jnp.zeros((CONV_HALO, cbuf_ref.shape[1]), f32)

    def conv_act(part):
        cols = slice(part * GDN_W, (part + 1) * GDN_W)
        cbuf_ref[CONV_HALO:, cols] = _dot(h_ref[...], wc_ref[:, cols])
        conv = cbuf_ref[CONV_HALO:, cols] * cw_ref[CONV_WIDTH - 1:CONV_WIDTH, cols]
        for tap in range(CONV_WIDTH - 1):
            back = CONV_WIDTH - 1 - tap
            conv = conv + (cbuf_ref[CONV_HALO - back:CONV_HALO - back + tm, cols]
                           * cw_ref[tap:tap + 1, cols])
        cbuf_ref[0:CONV_HALO, cols] = cbuf_ref[tm:tm + CONV_HALO, cols]
        return conv * jax.nn.sigmoid(conv)

    def l2norm_heads(act, o_ref, scale):
        for hd in range(HEADS):
            lanes = slice(hd * GDN_HEAD, (hd + 1) * GDN_HEAD)
            t = act[:, lanes]
            o_ref[:, lanes] = t * lax.rsqrt(jnp.sum(t * t, -1, keepdims=True) + RMS_EPS) * scale

    l2norm_heads(conv_act(0), gq_ref, GDN_HEAD ** -0.5)
    attn_group(wa_ref, fq_ref, fkt_ref, fv_ref, LOG2E * ATT_HEAD ** -0.5)
    l2norm_heads(conv_act(1), gk_ref, 1.0)
    attn_group(wb_ref, sq_ref, skt_ref, sv_ref, LOG2E * ATT_HEAD ** -0.5)
    gv_ref[...] = conv_act(2)
    sm_ref[...] = _dot(h_ref[...], ws_ref[...])


def _proj(x, mod, wa, wb, wc, ws, cw):
    bsz, s, d = x.shape
    tm = PROJ_TM
    row = lambda w: pl.BlockSpec((None, tm, w), lambda b, i: (b, i, 0))
    col = pl.BlockSpec((None, ATT_W, tm), lambda b, i: (b, 0, i))
    sd = jax.ShapeDtypeStruct
    return pl.pallas_call(
        _proj_kernel,
        grid=(bsz, s // tm),
        in_specs=[row(d), pl.BlockSpec((None, 3, d), lambda b, i: (b, 0, 0)),
                  _const_spec(wa.shape), _const_spec(wb.shape), _const_spec(wc.shape),
                  _const_spec(ws.shape), _const_spec(cw.shape)],
        out_specs=[row(ATT_W), col, row(ATT_W), row(ATT_W), col, row(ATT_W),
                   row(GDN_W), row(GDN_W), row(GDN_W), row(SMALL_W)],
        out_shape=[sd((bsz, s, ATT_W), bf16), sd((bsz, ATT_W, s), bf16), sd((bsz, s, ATT_W), bf16),
                   sd((bsz, s, ATT_W), bf16), sd((bsz, ATT_W, s), bf16), sd((bsz, s, ATT_W), bf16),
                   sd((bsz, s, GDN_W), f32), sd((bsz, s, GDN_W), f32), sd((bsz, s, GDN_W), f32),
                   sd((bsz, s, SMALL_W), f32)],
        scratch_shapes=[pltpu.VMEM((tm, d), bf16), pltpu.VMEM((tm + CONV_HALO, 3 * GDN_W), f32)],
        compiler_params=_params(("parallel", "arbitrary")),
        name="mixer_in_proj",
    )(x, mod, wa, wb, wc, ws, cw)


def _prefix_kernel(sm_ref, pv_ref, p_ref, pt_ref):
    s = sm_ref.shape[0]
    blk = PREFIX_BLK
    ri = lax.broadcasted_iota(jnp.int32, (blk, blk), 0)
    ci = lax.broadcasted_iota(jnp.int32, (blk, blk), 1)
    low = ri >= ci
    same = (ri // CHUNK) == (ci // CHUNK)
    tri_full = jnp.where(low, 1.0, 0.0).astype(bf16)
    tri_chunk = jnp.where(low & same, 1.0, 0.0).astype(bf16)
    ones_chunk = jnp.where(same, 1.0, 0.0).astype(bf16)
    lane = lax.broadcasted_iota(jnp.int32, (blk, SMALL_W), 1)
    b_forget = pv_ref[0:1, :]
    neg_a = -jnp.exp(pv_ref[1:2, :])
    dt_bias = pv_ref[2:3, :]
    carry = jnp.zeros((1, SMALL_W), f32)
    for i in range(s // blk):
        x = sm_ref[i * blk:(i + 1) * blk, :]
        xf = x + b_forget
        log_f = jnp.minimum(xf, 0.0) - _softplus_neg_abs(xf)
        beta = jax.nn.sigmoid(x)
        xg = x + dt_bias
        g = neg_a * (jnp.maximum(xg, 0.0) + _softplus_neg_abs(xg))
        val = jnp.where(lane < 4, log_f, jnp.where(lane < 8, beta, jnp.where(lane < 16, g, 0.0)))
        parts = _split3(val)
        cum_full = sum(_dot(tri_full, p) for p in parts) + carry
        cum_chunk = sum(_dot(tri_chunk, p) for p in parts)
        tot_chunk = sum(_dot(ones_chunk, p) for p in parts)
        carry = cum_full[blk - 1:blk, :]
        out = jnp.where(lane < 4, cum_full,
                        jnp.where(lane < 8, val, jnp.where(lane < 12, cum_chunk, tot_chunk)))
        p_ref[i * blk:(i + 1) * blk, :] = out
        pt_ref[:, i * blk:(i + 1) * blk] = out.T[0:16, :]


def _prefix(sm, pv):
    bsz, s, _ = sm.shape
    return pl.pallas_call(
        _prefix_kernel,
        grid=(bsz,),
        in_specs=[pl.BlockSpec((None, s, SMALL_W), lambda b: (b, 0, 0)), _const_spec(pv.shape)],
        out_specs=[pl.BlockSpec((None, s, SMALL_W), lambda b: (b, 0, 0)),
                   pl.BlockSpec((None, 16, s), lambda b: (b, 0, 0))],
        out_shape=[jax.ShapeDtypeStruct((bsz, s, SMALL_W), f32),
                   jax.ShapeDtypeStruct((bsz, 16, s), f32)],
        compiler_params=_params(("parallel",)),
        name="gate_prefix",
    )(sm, pv)


def _head_masks(shape, axis):
    lane = lax.broadcasted_iota(jnp.int32, shape, axis)
    return [(lane >= hd * ATT_HEAD) & (lane < (hd + 1) * ATT_HEAD) for hd in range(HEADS)]


def _by_head(masks, cols):
    out = cols[HEADS - 1]
    for hd in range(HEADS - 2, -1, -1):
        out = jnp.where(masks[hd], cols[hd], out)
    return out


def _fox_kernel(q_ref, kt_ref, v_ref, p_ref, pt_ref, o_ref,
                vm_ref, s_ref, cq_ref, m_ref, l_ref, acc_ref):
    tq, tk, lw = FOX_TQ, FOX_TK, LANES
    hs = range(HEADS)
    groups = range(tk // lw)
    i = pl.program_id(1)
    q = q_ref[...]
    qmask = _head_masks((tq, ATT_W), 1)
    zero = jnp.zeros((), bf16)
    qm = [jnp.where(qmask[hd], q, zero) for hd in hs]

    @pl.when(i == 0)
    def _():
        vmask = _head_masks(v_ref.shape, 1)
        v = v_ref[...]
        for hd in hs:
            vm_ref[hd] = jnp.where(vmask[hd], v, zero)

    delta = (lax.broadcasted_iota(jnp.int32, (tq, lw), 1)
             - lax.broadcasted_iota(jnp.int32, (tq, lw), 0))
    n_full = (i * tq) // tk

    def score_block(j, _, masked=False):
        off = pl.multiple_of(j * tk, tk)
        kt = kt_ref[:, pl.ds(off, tk)]
        for hd in hs:
            s = _dot(qm[hd], kt)
            cq = cq_ref[hd]
            ck = pt_ref[hd:hd + 1, pl.ds(off, tk)] * LOG2E
            mx = m_ref[hd]
            for g in groups:
                sg = s[:, g * lw:(g + 1) * lw] + cq - ck[:, g * lw:(g + 1) * lw]
                if masked:
                    sg = jnp.where(delta <= i * tq - off - g * lw, sg, NEG_BIG)
                s_ref[hd, :, pl.ds(off + g * lw, lw)] = sg
                mx = jnp.maximum(mx, sg)
            m_ref[hd] = mx
        return 0

    m_ref[...] = jnp.full(m_ref.shape, NEG_BIG, f32)
    for hd in hs:
        cq_ref[hd] = jnp.broadcast_to(p_ref[:, hd:hd + 1], (tq, lw)) * LOG2E
    lax.fori_loop(0, n_full, score_block, 0)
    score_block(n_full, 0, masked=True)
    for hd in hs:
        m_ref[hd] = jnp.broadcast_to(jnp.max(m_ref[hd], axis=-1, keepdims=True), (tq, lw))

    def sum_block(j, _):
        off = pl.multiple_of(j * tk, tk)
        pv = None
        for hd in hs:
            mx = m_ref[hd]
            p = jnp.exp2(s_ref[hd, :, pl.ds(off, tk)] - jnp.concatenate([mx] * len(groups), axis=1))
            l = l_ref[hd]
            for g in groups:
                l = l + p[:, g * lw:(g + 1) * lw]
            l_ref[hd] = l
            c = _dot(p.astype(bf16), vm_ref[hd, pl.ds(off, tk), :])
            pv = c if pv is None else pv + c
        acc_ref[...] += pv
        return 0

    l_ref[...] = jnp.zeros(l_ref.shape, f32)
    acc_ref[...] = jnp.zeros(acc_ref.shape, f32)
    lax.fori_loop(0, n_full + 1, sum_block, 0)
    l = [jnp.sum(l_ref[hd], axis=-1, keepdims=True) for hd in hs]
    o_ref[...] = (acc_ref[...] / _by_head(qmask, l)).astype(bf16)


def _fox(q, kt, v, p, pt):
    bsz, s, _ = q.shape
    tq = FOX_TQ
    return pl.pallas_call(
        _fox_kernel,
        grid=(bsz, s // tq),
        in_specs=[pl.BlockSpec((None, tq, ATT_W), lambda b, i: (b, i, 0)),
                  pl.BlockSpec((None, ATT_W, s), lambda b, i: (b, 0, 0)),
                  pl.BlockSpec((None, s, ATT_W), lambda b, i: (b, 0, 0)),
                  pl.BlockSpec((None, tq, SMALL_W), lambda b, i: (b, i, 0)),
                  pl.BlockSpec((None, 16, s), lambda b, i: (b, 0, 0))],
        out_specs=pl.BlockSpec((None, tq, ATT_W), lambda b, i: (b, i, 0)),
        out_shape=jax.ShapeDtypeStruct((bsz, s, ATT_W), bf16),
        scratch_shapes=[pltpu.VMEM((HEADS, s, ATT_W), bf16),
                        pltpu.VMEM((HEADS, tq, s), f32),
                        pltpu.VMEM((HEADS, tq, LANES), f32),
                        pltpu.VMEM((HEADS, tq, LANES), f32),
                        pltpu.VMEM((HEADS, tq, LANES), f32),
                        pltpu.VMEM((tq, ATT_W), f32)],
        compiler_params=_params(("parallel", "arbitrary")),
        name="forgetting_attention",
    )(q, kt, v, p, pt)


def _sb_kernel(q_ref, kt_ref, v_ref, o_ref):
    tq, tk = ATT_TQ, ATT_TK
    i = pl.program_id(1)
    q = q_ref[...]
    qmask = _head_masks((tq, ATT_W), 1)
    vmask = _head_masks((tk, ATT_W), 1)
    zero = jnp.zeros((), bf16)
    qm = [jnp.where(qmask[hd], q, zero) for hd in range(HEADS)]
    rows = lax.broadcasted_iota(jnp.int32, (tq, tk), 0)
    cols = lax.broadcasted_iota(jnp.int32, (tq, tk), 1)
    strict = cols < rows
    kr = lax.broadcasted_iota(jnp.int32, (tk, tk), 0)
    kc = lax.broadcasted_iota(jnp.int32, (tk, tk), 1)
    later = jnp.where(kr > kc, 1.0, 0.0).astype(bf16)

    def block(j, rem, acc, diag):
        off = pl.multiple_of(j * tk, tk)
        kt = kt_ref[:, pl.ds(off, tk)]
        v = v_ref[pl.ds(off, tk), :]
        hs = range(HEADS)
        z = [_dot(qm[hd], kt) for hd in hs]
        log_beta = [jnp.minimum(z[hd], 0.0) - jnp.log2(1.0 + jnp.exp2(-jnp.abs(z[hd])))
                    for hd in hs]
        log_rem = [log_beta[hd] - z[hd] for hd in hs]
        if diag:
            log_rem = [jnp.where(strict, log_rem[hd], 0.0) for hd in hs]
        parts = [_split2(log_rem[hd]) for hd in hs]
        after = [_dot(parts[hd][0], later) + _dot(parts[hd][1], later) + rem[hd] for hd in hs]
        w = [jnp.exp2(log_beta[hd] + after[hd]) for hd in hs]
        if diag:
            w = [jnp.where(strict, w[hd], 0.0) for hd in hs]
        rem_new = [rem[hd] + jnp.sum(log_rem[hd], axis=-1, keepdims=True) for hd in hs]
        for hd in hs:
            acc = acc + _dot(w[hd].astype(bf16), jnp.where(vmask[hd], v, zero))
        return tuple(rem_new), acc

    rem, acc = block(i, tuple(jnp.zeros((tq, 1), f32) for _ in range(HEADS)),
                     jnp.zeros((tq, ATT_W), f32), diag=True)

    def live(rem):
        top = rem[0]
        for hd in range(1, HEADS):
            top = jnp.maximum(top, rem[hd])
        return jnp.max(top) > EXP2_UNDERFLOW

    def cond(state):
        return jnp.logical_and(state[0] >= 0, state[1])

    def body(state):
        j, _, rem, acc = state
        rem, acc = block(j, rem, acc, diag=False)
        return j - 1, live(rem), rem, acc

    _, _, _, acc = lax.while_loop(cond, body, (i - 1, live(rem), rem, acc))
    o_ref[...] = acc.astype(bf16)


def _sb(q, kt, v):
    bsz, s, _ = q.shape
    tq = ATT_TQ
    return pl.pallas_call(
        _sb_kernel,
        grid=(bsz, s // tq),
        in_specs=[pl.BlockSpec((None, tq, ATT_W), lambda b, i: (b, i, 0)),
                  pl.BlockSpec((None, ATT_W, s), lambda b, i: (b, 0, 0)),
                  pl.BlockSpec((None, s, ATT_W), lambda b, i: (b, 0, 0))],
        out_specs=pl.BlockSpec((None, tq, ATT_W), lambda b, i: (b, i, 0)),
        out_shape=jax.ShapeDtypeStruct((bsz, s, ATT_W), bf16),
        compiler_params=_params(("parallel", "parallel")),
        name="stick_breaking_attention",
    )(q, kt, v)


def _gdn_kernel(q_ref, k_ref, v_ref, p_ref, pt_ref, ng_ref, o_ref,
                qe_ref, ol_ref, g_ref, b_ref, dec_ref, state_ref):
    ts = q_ref.shape[0]
    sb = GDN_SB
    hs = range(HEADS)
    lanes = [slice(hd * GDN_HEAD, (hd + 1) * GDN_HEAD) for hd in hs]
    ri = lax.broadcasted_iota(jnp.int32, (sb, sb), 0)
    ci = lax.broadcasted_iota(jnp.int32, (sb, sb), 1)
    same = (ri // CHUNK) == (ci // CHUNK)
    incl = same & (ri >= ci)
    strict = same & (ri > ci)
    cast = lambda t: t.astype(bf16)

    @pl.when(pl.program_id(1) == 0)
    def _():
        state_ref[...] = jnp.zeros(state_ref.shape, f32)

    def local(i, _):
        units = [(blk, h) for blk in range(GDN_LOCAL_BLOCKS) for h in hs]
        us = range(len(units))
        bi = [i * GDN_LOCAL_BLOCKS + blk for blk, _ in units]
        rows = [pl.ds(pl.multiple_of(bi[u] * sb, sb), sb) for u in us]
        hd = [h for _, h in units]
        k = [k_ref[rows[u], lanes[hd[u]]] for u in us]
        beta = [p_ref[rows[u], 4 + hd[u]:5 + hd[u]] for u in us]
        gc = [p_ref[rows[u], 8 + hd[u]:9 + hd[u]] for u in us]
        gl = [p_ref[rows[u], 12 + hd[u]:13 + hd[u]] for u in us]
        decay = [jnp.where(incl, jnp.exp(jnp.where(
            incl, gc[u] - pt_ref[8 + hd[u]:9 + hd[u], rows[u]], 0.0)), 0.0) for u in us]
        kb = [k[u] * beta[u] for u in us]
        kbf = [cast(k[u]) for u in us]
        a = [jnp.where(strict, _dot_nt(cast(kb[u]), kbf[u]) * decay[u], 0.0) for u in us]
        ab = [cast(a[u]) for u in us]
        pw = [_dot(ab[u], ab[u]) for u in us]
        n = [-a[u] for u in us]
        for level in range(5):
            pb = [cast(pw[u]) for u in us]
            n = [n[u] + pw[u] + _dot(cast(n[u]), pb[u]) for u in us]
            if level < 4:
                pw = [_dot(pb[u], pb[u]) for u in us]
        e_gc = [jnp.exp(gc[u]) for u in us]
        rhs = [jnp.concatenate([v_ref[rows[u], lanes[hd[u]]] * beta[u], kb[u] * e_gc[u]], axis=1)
               for u in us]
        sol = [rhs[u] + _dot(cast(n[u]), cast(rhs[u])) for u in us]
        solb = [cast(sol[u]) for u in us]
        q = [q_ref[rows[u], lanes[hd[u]]] for u in us]
        intra = [cast(jnp.where(incl, _dot_nt(cast(q[u]), kbf[u]) * decay[u], 0.0)) for u in us]
        prod = [_dot(intra[u], solb[u]) for u in us]
        kd = [cast(k[u] * jnp.exp(gl[u] - gc[u])) for u in us]
        for u in us:
            h = hd[u]
            ol_ref[rows[u], lanes[h]] = prod[u][:, :GDN_HEAD]
            qe_ref[rows[u], lanes[h]] = cast(q[u] * e_gc[u] - prod[u][:, GDN_HEAD:])
            glb = jnp.exp(jnp.broadcast_to(gl[u], (sb, GDN_HEAD)))
            for c in range(2):
                crow = slice(c * CHUNK, (c + 1) * CHUNK)
                bg = _dot_tn(kd[u][crow], solb[u][crow])
                chunk = 2 * bi[u] + c
                grow = pl.ds(pl.multiple_of(chunk * GDN_HEAD, GDN_HEAD), GDN_HEAD)
                b_ref[h, grow, :] = bg[:, :GDN_HEAD]
                g_ref[h, grow, :] = cast(bg[:, GDN_HEAD:])
                dec_ref[h, pl.ds(chunk, 1), :] = glb[c * CHUNK:c * CHUNK + 1, :]
        return 0

    lax.fori_loop(0, ts // (sb * GDN_LOCAL_BLOCKS), local, 0)

    def scan(c, _):
        crow = pl.ds(pl.multiple_of(c * CHUNK, CHUNK), CHUNK)
        grow = pl.ds(pl.multiple_of(c * GDN_HEAD, GDN_HEAD), GDN_HEAD)
        state = [state_ref[h] for h in hs]
        sbf = [cast(state[h]) for h in hs]
        gs = [_dot(g_ref[h, grow, :], sbf[h]) for h in hs]
        for h in hs:
            state_ref[h] = state[h] * dec_ref[h, pl.ds(c, 1), :] - gs[h] + b_ref[h, grow, :]
        for h in hs:
            o_ref[crow, lanes[h]] = _dot(qe_ref[crow, lanes[h]], sbf[h]) + ol_ref[crow, lanes[h]]
        return 0

    lax.fori_loop(0, ts // CHUNK, scan, 0)

    def norm(i, _):
        rows = pl.ds(pl.multiple_of(i * GDN_NORM_ROWS, GDN_NORM_ROWS), GDN_NORM_ROWS)
        for h in hs:
            o = o_ref[rows, lanes[h]]
            o_ref[rows, lanes[h]] = (o * lax.rsqrt(jnp.mean(o * o, axis=-1, keepdims=True) + RMS_EPS)
                                     * ng_ref[...])
        return 0

    lax.fori_loop(0, ts // GDN_NORM_ROWS, norm, 0)


def _gdn(q, k, v, p, pt, ng):
    bsz, s, _ = q.shape
    ts = min(GDN_TS, s)
    n_chunks = ts // CHUNK
    row = lambda w: pl.BlockSpec((None, ts, w), lambda b, i: (b, i, 0))
    return pl.pallas_call(
        _gdn_kernel,
        grid=(bsz, s // ts),
        in_specs=[row(GDN_W), row(GDN_W), row(GDN_W), row(SMALL_W),
                  pl.BlockSpec((None, 16, ts), lambda b, i: (b, 0, i)), _const_spec(ng.shape)],
        out_specs=row(GDN_W),
        out_shape=jax.ShapeDtypeStruct((bsz, s, GDN_W), f32),
        scratch_shapes=[pltpu.VMEM((ts, GDN_W), bf16), pltpu.VMEM((ts, GDN_W), f32),
                        pltpu.VMEM((HEADS, n_chunks * GDN_HEAD, GDN_HEAD), bf16),
                        pltpu.VMEM((HEADS, n_chunks * GDN_HEAD, GDN_HEAD), f32),
                        pltpu.VMEM((HEADS, n_chunks, GDN_HEAD), f32),
                        pltpu.VMEM((HEADS, GDN_HEAD, GDN_HEAD), f32)],
        compiler_params=_params(("parallel", "arbitrary")),
        name="gated_delta_rule",
    )(q, k, v, p, pt, ng)


def _merge_kernel(x_ref, mod_ref, oa_ref, ob_ref, on_ref, wgate_ref, wgg_ref,
                  wba_ref, wbb_ref, wbc_ref, wo_ref, lng_ref, lnb_ref, o_ref, h_ref):
    x = x_ref[...]
    d = x.shape[1]
    h_ref[...] = (x * (1.0 + mod_ref[1:2, :]) + mod_ref[0:1, :]).astype(bf16)
    gg = _dot(h_ref[...], wgg_ref[...])
    oc = (on_ref[...] * (gg * jax.nn.sigmoid(gg))).astype(bf16)
    branches = ((oa_ref[...], wba_ref), (ob_ref[...], wbb_ref), (oc, wbc_ref))
    merged = None
    for n, (o, wb_ref) in enumerate(branches):
        gate = jax.nn.sigmoid(_dot(h_ref[...], wgate_ref[:, n * d:(n + 1) * d]))
        term = gate * _dot(o, wb_ref[...])
        merged = term if merged is None else merged + term
    y = _dot(merged.astype(bf16), wo_ref[...])
    r = DEEPNORM_ALPHA * x + (1.0 + mod_ref[2:3, :]) * y
    o_ref[...] = _layer_norm(r, lng_ref[...], lnb_ref[...])


def _merge(x, mod, oa, ob, on, wgate, wgg, wba, wbb, wbc, wo, lng, lnb):
    bsz, s, d = x.shape
    tm = MERGE_TM
    row = lambda w: pl.BlockSpec((None, tm, w), lambda b, i: (b, i, 0))
    return pl.pallas_call(
        _merge_kernel,
        grid=(bsz, s // tm),
        in_specs=[row(d), pl.BlockSpec((None, 3, d), lambda b, i: (b, 0, 0)),
                  row(ATT_W), row(ATT_W), row(GDN_W),
                  _const_spec(wgate.shape), _const_spec(wgg.shape), _const_spec(wba.shape),
                  _const_spec(wbb.shape), _const_spec(wbc.shape), _const_spec(wo.shape),
                  _const_spec((1, d)), _const_spec((1, d))],
        out_specs=row(d),
        out_shape=jax.ShapeDtypeStruct(x.shape, f32),
        scratch_shapes=[pltpu.VMEM((tm, d), bf16)],
        compiler_params=_params(("parallel", "parallel")),
        name="mixer_merge",
    )(x, mod, oa, ob, on, wgate, wgg, wba, wbb, wbc, wo, lng, lnb)


def _mixer_weights(w_in, b_forget, a_log, dt_bias):
    d = w_in.shape[0]
    sizes = (ATT_W, ATT_W, ATT_W, HEADS, ATT_W, ATT_W, ATT_W, 3 * GDN_W, HEADS, HEADS, GDN_W,
             3 * D_MODEL)
    offs = [0]
    for n in sizes:
        offs.append(offs[-1] + n)
    piece = lambda n: w_in[:, offs[n]:offs[n + 1]]
    wa = jnp.concatenate([piece(0), piece(1), piece(2)], axis=1).astype(bf16)
    wb = jnp.concatenate([piece(4), piece(5), piece(6)], axis=1).astype(bf16)
    wc = piece(7).astype(bf16)
    ws = jnp.concatenate([piece(3), piece(8), piece(9), piece(9),
                          jnp.zeros((d, SMALL_W - 4 * HEADS), w_in.dtype)], axis=1).astype(bf16)
    wgg = piece(10).astype(bf16)
    wgate = piece(11).astype(bf16)
    pad = lambda vec, at: jnp.zeros((SMALL_W,), f32).at[at:at + HEADS].set(vec.astype(f32))
    pv = jnp.stack([pad(b_forget, 0),
                    pad(a_log, 2 * HEADS) + pad(a_log, 3 * HEADS),
                    pad(dt_bias, 2 * HEADS) + pad(dt_bias, 3 * HEADS)]
                   + [jnp.zeros((SMALL_W,), f32)] * 5)
    return wa, wb, wc, ws, wgg, wgate, pv


def kernel(x, c, w_ada, b_ada, ln_g, ln_b, ffn_w_up, ffn_w_down, w_in, b_forget, conv_w,
           gdn_a_log, gdn_dt_bias, gdn_norm_g, w_branch, w_o):
    bsz = x.shape[0]
    mod = _ada(c, w_ada, b_ada).reshape(DEPTH, bsz, N_SUB, 3, D_MODEL)
    wup = ffn_w_up.astype(bf16)
    wdown = ffn_w_down.astype(bf16)
    for l in range(DEPTH):
        lng = lambda n: ln_g[l, n].reshape(1, D_MODEL)
        lnb = lambda n: ln_b[l, n].reshape(1, D_MODEL)
        x = _ffn(x, mod[l, :, 0], wup[l, 0], wdown[l, 0], lng(0), lnb(0))

        wa, wb, wc, ws, wgg, wgate, pv = _mixer_weights(w_in[l], b_forget[l], gdn_a_log[l],
                                                        gdn_dt_bias[l])
        m = mod[l, :, 1]
        fq, fkt, fv, sq, skt, sv, gq, gk, gv, sm = _proj(x, m, wa, wb, wc, ws, conv_w[l])
        p, pt = _prefix(sm, pv)
        oa = _fox(fq, fkt, fv, p, pt)
        ob = _sb(sq, skt, sv)
        on = _gdn(gq, gk, gv, p, pt, gdn_norm_g[l].reshape(1, GDN_HEAD))
        wbr = w_branch[l].astype(bf16)
        x = _merge(x, m, oa, ob, on, wgate, wgg, wbr[:ATT_W], wbr[ATT_W:2 * ATT_W],
                   wbr[2 * ATT_W:], w_o[l].astype(bf16), lng(1), lnb(1))

        x = _ffn(x, mod[l, :, 2], wup[l, 1], wdown[l, 1], lng(2), lnb(2))
    return x
```

```python
import functools

import jax
import jax.numpy as jnp
from jax import lax
from jax.experimental import pallas as pl
from jax.experimental.pallas import tpu as pltpu

f32 = jnp.float32
bf16 = jnp.bfloat16

D_MODEL = 1024
DEPTH = 2
N_SUB = 3
FFN_HIDDEN = 2816
HEADS = 4
ATT_W = 256
ATT_HEAD = 64
GDN_W = 512
GDN_HEAD = 128
CHUNK = 64
CONV_WIDTH = 4
SMALL_W = 128
DEEPNORM_ALPHA = (2.0 * DEPTH) ** 0.25
LN_EPS = 1e-5
RMS_EPS = 1e-6
LOG2E = 1.4426950408889634

FFN_TM = 512
FFN_HC = 256
PROJ_TM = 512
MERGE_TM = 512
LANES = 128
ATT_TQ = 256
ATT_TK = 256
FOX_TQ = 512
FOX_TK = 512
PREFIX_BLK = 256
GDN_SB = 128
GDN_TS = 1024
GDN_LOCAL_BLOCKS = 4
GDN_NORM_ROWS = 256
CONV_HALO = 8
NEG_BIG = -1e30
EXP2_UNDERFLOW = -151.0
VMEM_LIMIT = 56 * 1024 * 1024


def _dot(a, b):
    return jnp.dot(a, b, preferred_element_type=f32)


def _dot_nt(a, b):
    return lax.dot_general(a, b, (((1,), (1,)), ((), ())), preferred_element_type=f32)


def _dot_tn(a, b):
    return lax.dot_general(a, b, (((0,), (0,)), ((), ())), preferred_element_type=f32)


def _split2(x):
    hi = x.astype(bf16)
    lo = (x - hi.astype(f32)).astype(bf16)
    return hi, lo


def _split3(x):
    hi = x.astype(bf16)
    r = x - hi.astype(f32)
    mid = r.astype(bf16)
    lo = (r - mid.astype(f32)).astype(bf16)
    return hi, mid, lo


def _dot3(a, b):
    ah, al = _split2(a)
    bh, bl = _split2(b)
    return _dot(ah, bh) + (_dot(ah, bl) + _dot(al, bh))


def _layer_norm(r, g, b):
    mu = jnp.mean(r, axis=-1, keepdims=True)
    d = r - mu
    var = jnp.mean(d * d, axis=-1, keepdims=True)
    return d * lax.rsqrt(var + LN_EPS) * g + b


def _softplus_neg_abs(x):
    return jnp.log1p(jnp.exp(-jnp.abs(x)))


def _const_spec(shape):
    nd = len(shape)
    return pl.BlockSpec(shape, lambda *_: (0,) * nd, pipeline_mode=pl.Buffered(1))


def _params(sem):
    return pltpu.CompilerParams(dimension_semantics=sem, vmem_limit_bytes=VMEM_LIMIT)


def _ada_kernel(c_ref, w_ref, b_ref, o_ref):
    c = c_ref[...]
    sc = c * jax.nn.sigmoid(c)
    o_ref[...] = _dot3(sc, w_ref[...]) + b_ref[...]


def _ada(c, w_ada, b_ada):
    depth, d, n = w_ada.shape
    bsz = c.shape[0]
    tn = 1024
    return pl.pallas_call(
        _ada_kernel,
        grid=(depth, n // tn),
        in_specs=[pl.BlockSpec((bsz, d), lambda l, j: (0, 0)),
                  pl.BlockSpec((None, d, tn), lambda l, j: (l, 0, j)),
                  pl.BlockSpec((None, 1, tn), lambda l, j: (l, 0, j))],
        out_specs=pl.BlockSpec((None, bsz, tn), lambda l, j: (l, 0, j)),
        out_shape=jax.ShapeDtypeStruct((depth, bsz, n), f32),
        compiler_params=_params(("parallel", "parallel")),
        name="adaln",
    )(c, w_ada, b_ada.reshape(depth, 1, n))


def _ffn_kernel(x_ref, mod_ref, wup_ref, wd_ref, lng_ref, lnb_ref, o_ref, h_ref, acc_ref):
    x = x_ref[...]
    h_ref[...] = (x * (1.0 + mod_ref[1:2, :]) + mod_ref[0:1, :]).astype(bf16)
    for j in range(FFN_HIDDEN // FFN_HC):
        lo = j * FFN_HC
        h = h_ref[...]
        g = _dot(h, wup_ref[:, lo:lo + FFN_HC])
        u = _dot(h, wup_ref[:, FFN_HIDDEN + lo:FFN_HIDDEN + lo + FFN_HC])
        a = (g * jax.nn.sigmoid(g) * u).astype(bf16)
        y = _dot(a, wd_ref[lo:lo + FFN_HC, :])
        if j == 0:
            acc_ref[...] = y
        else:
            acc_ref[...] += y
    r = DEEPNORM_ALPHA * x + (0.5 * (1.0 + mod_ref[2:3, :])) * acc_ref[...]
    o_ref[...] = _layer_norm(r, lng_ref[...], lnb_ref[...])


def _ffn(x, mod, wup, wd, lng, lnb):
    bsz, s, d = x.shape
    tm = FFN_TM
    return pl.pallas_call(
        _ffn_kernel,
        grid=(bsz, s // tm),
        in_specs=[pl.BlockSpec((None, tm, d), lambda b, i: (b, i, 0)),
                  pl.BlockSpec((None, 3, d), lambda b, i: (b, 0, 0)),
                  _const_spec(wup.shape), _const_spec(wd.shape),
                  _const_spec((1, d)), _const_spec((1, d))],
        out_specs=pl.BlockSpec((None, tm, d), lambda b, i: (b, i, 0)),
        out_shape=jax.ShapeDtypeStruct(x.shape, f32),
        scratch_shapes=[pltpu.VMEM((tm, d), bf16), pltpu.VMEM((tm, d), f32)],
        compiler_params=_params(("parallel", "parallel")),
        name="ffn",
    )(x, mod, wup, wd, lng, lnb)


def _proj_kernel(x_ref, mod_ref, wa_ref, wb_ref, wc_ref, ws_ref, cw_ref,
                 fq_ref, fkt_ref, fv_ref, sq_ref, skt_ref, sv_ref,
                 gq_ref, gk_ref, gv_ref, sm_ref, h_ref, cbuf_ref):
    tm = x_ref.shape[0]
    h_ref[...] = (x_ref[...] * (1.0 + mod_ref[1:2, :]) + mod_ref[0:1, :]).astype(bf16)

    def attn_group(w_ref, q_ref, kt_ref, v_ref, q_scale):
        r = _dot(h_ref[...], w_ref[...])
        q_ref[...] = (r[:, :ATT_W] * q_scale).astype(bf16)
        kt_ref[...] = r[:, ATT_W:2 * ATT_W].T.astype(bf16)
        v_ref[...] = r[:, 2 * ATT_W:].astype(bf16)

    @pl.when(pl.program_id(1) == 0)
    def _():
        cbuf_ref[0:CONV_HALO, :] = jnp.zeros((CONV_HALO, cbuf_ref.shape[1]), f32)

    def conv_act(part):
        cols = slice(part * GDN_W, (part + 1) * GDN_W)
        cbuf_ref[CONV_HALO:, cols] = _dot(h_ref[...], wc_ref[:, cols])
        conv = cbuf_ref[CONV_HALO:, cols] * cw_ref[CONV_WIDTH - 1:CONV_WIDTH, cols]
        for tap in range(CONV_WIDTH - 1):
            back = CONV_WIDTH - 1 - tap
            conv = conv + (cbuf_ref[CONV_HALO - back:CONV_HALO - back + tm, cols]
                           * cw_ref[tap:tap + 1, cols])
        cbuf_ref[0:CONV_HALO, cols] = cbuf_ref[tm:tm + CONV_HALO, cols]
        return conv * jax.nn.sigmoid(conv)

    def l2norm_heads(act, o_ref, scale):
        for hd in range(HEADS):
            lanes = slice(hd * GDN_HEAD, (hd + 1) * GDN_HEAD)
            t = act[:, lanes]
            o_ref[hd] = t * lax.rsqrt(jnp.sum(t * t, -1, keepdims=True) + RMS_EPS) * scale

    l2norm_heads(conv_act(0), gq_ref, GDN_HEAD ** -0.5)
    attn_group(wa_ref, fq_ref, fkt_ref, fv_ref, LOG2E * ATT_HEAD ** -0.5)
    l2norm_heads(conv_act(1), gk_ref, 1.0)
    attn_group(wb_ref, sq_ref, skt_ref, sv_ref, LOG2E * ATT_HEAD ** -0.5)
    act_v = conv_act(2)
    for hd in range(HEADS):
        gv_ref[hd] = act_v[:, hd * GDN_HEAD:(hd + 1) * GDN_HEAD]
    sm_ref[...] = _dot(h_ref[...], ws_ref[...])


def _proj(x, mod, wa, wb, wc, ws, cw):
    bsz, s, d = x.shape
    tm = PROJ_TM
    row = lambda w: pl.BlockSpec((None, tm, w), lambda b, i: (b, i, 0))
    col = pl.BlockSpec((None, ATT_W, tm), lambda b, i: (b, 0, i))
    per_head = pl.BlockSpec((None, HEADS, tm, GDN_HEAD), lambda b, i: (b, 0, i, 0))
    sd = jax.ShapeDtypeStruct
    return pl.pallas_call(
        _proj_kernel,
        grid=(bsz, s // tm),
        in_specs=[row(d), pl.BlockSpec((None, 3, d), lambda b, i: (b, 0, 0)),
                  _const_spec(wa.shape), _const_spec(wb.shape), _const_spec(wc.shape),
                  _const_spec(ws.shape), _const_spec(cw.shape)],
        out_specs=[row(ATT_W), col, row(ATT_W), row(ATT_W), col, row(ATT_W),
                   per_head, per_head, per_head, row(SMALL_W)],
        out_shape=[sd((bsz, s, ATT_W), bf16), sd((bsz, ATT_W, s), bf16), sd((bsz, s, ATT_W), bf16),
                   sd((bsz, s, ATT_W), bf16), sd((bsz, ATT_W, s), bf16), sd((bsz, s, ATT_W), bf16),
                   sd((bsz, HEADS, s, GDN_HEAD), f32), sd((bsz, HEADS, s, GDN_HEAD), f32),
                   sd((bsz, HEADS, s, GDN_HEAD), f32), sd((bsz, s, SMALL_W), f32)],
        scratch_shapes=[pltpu.VMEM((tm, d), bf16), pltpu.VMEM((tm + CONV_HALO, 3 * GDN_W), f32)],
        compiler_params=_params(("parallel", "arbitrary")),
        name="mixer_in_proj",
    )(x, mod, wa, wb, wc, ws, cw)


def _prefix_kernel(sm_ref, pv_ref, p_ref, pt_ref):
    s = sm_ref.shape[0]
    blk = PREFIX_BLK
    ri = lax.broadcasted_iota(jnp.int32, (blk, blk), 0)
    ci = lax.broadcasted_iota(jnp.int32, (blk, blk), 1)
    low = ri >= ci
    same = (ri // CHUNK) == (ci // CHUNK)
    tri_full = jnp.where(low, 1.0, 0.0).astype(bf16)
    tri_chunk = jnp.where(low & same, 1.0, 0.0).astype(bf16)
    ones_chunk = jnp.where(same, 1.0, 0.0).astype(bf16)
    lane = lax.broadcasted_iota(jnp.int32, (blk, SMALL_W), 1)
    b_forget = pv_ref[0:1, :]
    neg_a = -jnp.exp(pv_ref[1:2, :])
    dt_bias = pv_ref[2:3, :]
    carry = jnp.zeros((1, SMALL_W), f32)
    for i in range(s // blk):
        x = sm_ref[i * blk:(i + 1) * blk, :]
        xf = x + b_forget
        log_f = jnp.minimum(xf, 0.0) - _softplus_neg_abs(xf)
        beta = jax.nn.sigmoid(x)
        xg = x + dt_bias
        g = neg_a * (jnp.maximum(xg, 0.0) + _softplus_neg_abs(xg))
        val = jnp.where(lane < 4, log_f, jnp.where(lane < 8, beta, jnp.where(lane < 16, g, 0.0)))
        parts = _split3(val)
        cum_full = sum(_dot(tri_full, p) for p in parts) + carry
        cum_chunk = sum(_dot(tri_chunk, p) for p in parts)
        tot_chunk = sum(_dot(ones_chunk, p) for p in parts)
        carry = cum_full[blk - 1:blk, :]
        out = jnp.where(lane < 4, cum_full,
                        jnp.where(lane < 8, val, jnp.where(lane < 12, cum_chunk, tot_chunk)))
        p_ref[i * blk:(i + 1) * blk, :] = out
        pt_ref[:, i * blk:(i + 1) * blk] = out.T[0:16, :]


def _prefix(sm, pv):
    bsz, s, _ = sm.shape
    return pl.pallas_call(
        _prefix_kernel,
        grid=(bsz,),
        in_specs=[pl.BlockSpec((None, s, SMALL_W), lambda b: (b, 0, 0)), _const_spec(pv.shape)],
        out_specs=[pl.BlockSpec((None, s, SMALL_W), lambda b: (b, 0, 0)),
                   pl.BlockSpec((None, 16, s), lambda b: (b, 0, 0))],
        out_shape=[jax.ShapeDtypeStruct((bsz, s, SMALL_W), f32),
                   jax.ShapeDtypeStruct((bsz, 16, s), f32)],
        compiler_params=_params(("parallel",)),
        name="gate_prefix",
    )(sm, pv)


def _head_masks(shape, axis):
    lane = lax.broadcasted_iota(jnp.int32, shape, axis)
    return [(lane >= hd * ATT_HEAD) & (lane < (hd + 1) * ATT_HEAD) for hd in range(HEADS)]


def _by_head(masks, cols):
    out = cols[HEADS - 1]
    for hd in range(HEADS - 2, -1, -1):
        out = jnp.where(masks[hd], cols[hd], out)
    return out


def _fox_kernel(q_ref, kt_ref, v_ref, p_ref, pt_ref, o_ref,
                vm_ref, s_ref, cq_ref, m_ref, l_ref, acc_ref):
    tq, tk, lw = FOX_TQ, FOX_TK, LANES
    hs = range(HEADS)
    groups = range(tk // lw)
    i = pl.program_id(1)
    q = q_ref[...]
    qmask = _head_masks((tq, ATT_W), 1)
    zero = jnp.zeros((), bf16)
    qm = [jnp.where(qmask[hd], q, zero) for hd in hs]

    @pl.when(i == 0)
    def _():
        vmask = _head_masks(v_ref.shape, 1)
        v = v_ref[...]
        for hd in hs:
            vm_ref[hd] = jnp.where(vmask[hd], v, zero)

    delta = (lax.broadcasted_iota(jnp.int32, (tq, lw), 1)
             - lax.broadcasted_iota(jnp.int32, (tq, lw), 0))
    n_full = (i * tq) // tk

    def score_block(j, _, masked=False):
        off = pl.multiple_of(j * tk, tk)
        kt = kt_ref[:, pl.ds(off, tk)]
        for hd in hs:
            s = _dot(qm[hd], kt)
            cq = cq_ref[hd]
            ck = pt_ref[hd:hd + 1, pl.ds(off, tk)] * LOG2E
            mx = m_ref[hd]
            for g in groups:
                sg = s[:, g * lw:(g + 1) * lw] + cq - ck[:, g * lw:(g + 1) * lw]
                if masked:
                    sg = jnp.where(delta <= i * tq - off - g * lw, sg, NEG_BIG)
                s_ref[hd, j * len(groups) + g] = sg
                mx = jnp.maximum(mx, sg)
            m_ref[hd] = mx
        return 0

    m_ref[...] = jnp.full(m_ref.shape, NEG_BIG, f32)
    for hd in hs:
        cq_ref[hd] = jnp.broadcast_to(p_ref[:, hd:hd + 1], (tq, lw)) * LOG2E
    lax.fori_loop(0, n_full, score_block, 0)
    score_block(n_full, 0, masked=True)
    for hd in hs:
        m_ref[hd] = jnp.broadcast_to(jnp.max(m_ref[hd], axis=-1, keepdims=True), (tq, lw))

    def sum_block(j, _):
        off = pl.multiple_of(j * tk, tk)
        pv = None
        for hd in hs:
            mx = m_ref[hd]
            parts = [jnp.exp2(s_ref[hd, j * len(groups) + g] - mx) for g in groups]
            l = l_ref[hd]
            for g in groups:
                l = l + parts[g]
            l_ref[hd] = l
            p = jnp.concatenate(parts, axis=1)
            c = _dot(p.astype(bf16), vm_ref[hd, pl.ds(off, tk), :])
            pv = c if pv is None else pv + c
        acc_ref[...] += pv
        return 0

    l_ref[...] = jnp.zeros(l_ref.shape, f32)
    acc_ref[...] = jnp.zeros(acc_ref.shape, f32)
    lax.fori_loop(0, n_full + 1, sum_block, 0)
    l = [jnp.sum(l_ref[hd], axis=-1, keepdims=True) for hd in hs]
    o_ref[...] = (acc_ref[...] / _by_head(qmask, l)).astype(bf16)


def _fox(q, kt, v, p, pt):
    bsz, s, _ = q.shape
    tq = FOX_TQ
    return pl.pallas_call(
        _fox_kernel,
        grid=(bsz, s // tq),
        in_specs=[pl.BlockSpec((None, tq, ATT_W), lambda b, i: (b, i, 0)),
                  pl.BlockSpec((None, ATT_W, s), lambda b, i: (b, 0, 0)),
                  pl.BlockSpec((None, s, ATT_W), lambda b, i: (b, 0, 0)),
                  pl.BlockSpec((None, tq, SMALL_W), lambda b, i: (b, i, 0)),
                  pl.BlockSpec((None, 16, s), lambda b, i: (b, 0, 0))],
        out_specs=pl.BlockSpec((None, tq, ATT_W), lambda b, i: (b, i, 0)),
        out_shape=jax.ShapeDtypeStruct((bsz, s, ATT_W), bf16),
        scratch_shapes=[pltpu.VMEM((HEADS, s, ATT_W), bf16),
                        pltpu.VMEM((HEADS, s // LANES, tq, LANES), f32),
                        pltpu.VMEM((HEADS, tq, LANES), f32),
                        pltpu.VMEM((HEADS, tq, LANES), f32),
                        pltpu.VMEM((HEADS, tq, LANES), f32),
                        pltpu.VMEM((tq, ATT_W), f32)],
        compiler_params=_params(("parallel", "arbitrary")),
        name="forgetting_attention",
    )(q, kt, v, p, pt)


def _sb_kernel(q_ref, kt_ref, v_ref, o_ref):
    tq, tk = ATT_TQ, ATT_TK
    i = pl.program_id(1)
    q = q_ref[...]
    qmask = _head_masks((tq, ATT_W), 1)
    vmask = _head_masks((tk, ATT_W), 1)
    zero = jnp.zeros((), bf16)
    qm = [jnp.where(qmask[hd], q, zero) for hd in range(HEADS)]
    rows = lax.broadcasted_iota(jnp.int32, (tq, tk), 0)
    cols = lax.broadcasted_iota(jnp.int32, (tq, tk), 1)
    strict = cols < rows
    kr = lax.broadcasted_iota(jnp.int32, (tk, tk), 0)
    kc = lax.broadcasted_iota(jnp.int32, (tk, tk), 1)
    later = jnp.where(kr > kc, 1.0, 0.0).astype(bf16)

    def block(j, rem, acc, diag):
        off = pl.multiple_of(j * tk, tk)
        kt = kt_ref[:, pl.ds(off, tk)]
        v = v_ref[pl.ds(off, tk), :]
        hs = range(HEADS)
        z = [_dot(qm[hd], kt) for hd in hs]
        log_beta = [jnp.minimum(z[hd], 0.0) - jnp.log2(1.0 + jnp.exp2(-jnp.abs(z[hd])))
                    for hd in hs]
        log_rem = [log_beta[hd] - z[hd] for hd in hs]
        if diag:
            log_rem = [jnp.where(strict, log_rem[hd], 0.0) for hd in hs]
        parts = [_split2(log_rem[hd]) for hd in hs]
        after = [_dot(parts[hd][0], later) + _dot(parts[hd][1], later) + rem[hd] for hd in hs]
        w = [jnp.exp2(log_beta[hd] + after[hd]) for hd in hs]
        if diag:
            w = [jnp.where(strict, w[hd], 0.0) for hd in hs]
        rem_new = [rem[hd] + jnp.sum(log_rem[hd], axis=-1, keepdims=True) for hd in hs]
        for hd in hs:
            acc = acc + _dot(w[hd].astype(bf16), jnp.where(vmask[hd], v, zero))
        return tuple(rem_new), acc

    rem, acc = block(i, tuple(jnp.zeros((tq, 1), f32) for _ in range(HEADS)),
                     jnp.zeros((tq, ATT_W), f32), diag=True)

    def live(rem):
        top = rem[0]
        for hd in range(1, HEADS):
            top = jnp.maximum(top, rem[hd])
        return jnp.max(top) > EXP2_UNDERFLOW

    def cond(state):
        return jnp.logical_and(state[0] >= 0, state[1])

    def body(state):
        j, _, rem, acc = state
        rem, acc = block(j, rem, acc, diag=False)
        return j - 1, live(rem), rem, acc

    _, _, _, acc = lax.while_loop(cond, body, (i - 1, live(rem), rem, acc))
    o_ref[...] = acc.astype(bf16)


def _sb(q, kt, v):
    bsz, s, _ = q.shape
    tq = ATT_TQ
    return pl.pallas_call(
        _sb_kernel,
        grid=(bsz, s // tq),
        in_specs=[pl.BlockSpec((None, tq, ATT_W), lambda b, i: (b, i, 0)),
                  pl.BlockSpec((None, ATT_W, s), lambda b, i: (b, 0, 0)),
                  pl.BlockSpec((None, s, ATT_W), lambda b, i: (b, 0, 0))],
        out_specs=pl.BlockSpec((None, tq, ATT_W), lambda b, i: (b, i, 0)),
        out_shape=jax.ShapeDtypeStruct((bsz, s, ATT_W), bf16),
        compiler_params=_params(("parallel", "parallel")),
        name="stick_breaking_attention",
    )(q, kt, v)


def _gdn_kernel(q_ref, k_ref, v_ref, p_ref, pt_ref, ng_ref, o_ref,
                qe_ref, ol_ref, g_ref, b_ref, dec_ref, state_ref):
    ts = q_ref.shape[1]
    sb = GDN_SB
    hs = range(HEADS)
    lanes = [slice(hd * GDN_HEAD, (hd + 1) * GDN_HEAD) for hd in hs]
    ri = lax.broadcasted_iota(jnp.int32, (sb, sb), 0)
    ci = lax.broadcasted_iota(jnp.int32, (sb, sb), 1)
    same = (ri // CHUNK) == (ci // CHUNK)
    incl = same & (ri >= ci)
    strict = same & (ri > ci)
    cast = lambda t: t.astype(bf16)

    @pl.when(pl.program_id(1) == 0)
    def _():
        state_ref[...] = jnp.zeros(state_ref.shape, f32)

    def local(i, _):
        units = [(blk, h) for blk in range(GDN_LOCAL_BLOCKS) for h in hs]
        us = range(len(units))
        bi = [i * GDN_LOCAL_BLOCKS + blk for blk, _ in units]
        rows = [pl.ds(pl.multiple_of(bi[u] * sb, sb), sb) for u in us]
        hd = [h for _, h in units]
        k = [k_ref[hd[u], rows[u], :] for u in us]
        beta = [p_ref[rows[u], 4 + hd[u]:5 + hd[u]] for u in us]
        gc = [p_ref[rows[u], 8 + hd[u]:9 + hd[u]] for u in us]
        gl = [p_ref[rows[u], 12 + hd[u]:13 + hd[u]] for u in us]
        decay = [jnp.where(incl, jnp.exp(jnp.where(
            incl, gc[u] - pt_ref[8 + hd[u]:9 + hd[u], rows[u]], 0.0)), 0.0) for u in us]
        kb = [k[u] * beta[u] for u in us]
        kbf = [cast(k[u]) for u in us]
        a = [jnp.where(strict, _dot_nt(cast(kb[u]), kbf[u]) * decay[u], 0.0) for u in us]
        ab = [cast(a[u]) for u in us]
        pw = [_dot(ab[u], ab[u]) for u in us]
        n = [-a[u] for u in us]
        for level in range(5):
            pb = [cast(pw[u]) for u in us]
            n = [n[u] + pw[u] + _dot(cast(n[u]), pb[u]) for u in us]
            if level < 4:
                pw = [_dot(pb[u], pb[u]) for u in us]
        e_gc = [jnp.exp(gc[u]) for u in us]
        rhs = [jnp.concatenate([v_ref[hd[u], rows[u], :] * beta[u], kb[u] * e_gc[u]], axis=1)
               for u in us]
        sol = [rhs[u] + _dot(cast(n[u]), cast(rhs[u])) for u in us]
        solb = [cast(sol[u]) for u in us]
        q = [q_ref[hd[u], rows[u], :] for u in us]
        intra = [cast(jnp.where(incl, _dot_nt(cast(q[u]), kbf[u]) * decay[u], 0.0)) for u in us]
        prod = [_dot(intra[u], solb[u]) for u in us]
        kd = [cast(k[u] * jnp.exp(gl[u] - gc[u])) for u in us]
        for u in us:
            h = hd[u]
            ol_ref[h, rows[u], :] = prod[u][:, :GDN_HEAD]
            qe_ref[h, rows[u], :] = cast(q[u] * e_gc[u] - prod[u][:, GDN_HEAD:])
            glb = jnp.exp(jnp.broadcast_to(gl[u], (sb, GDN_HEAD)))
            for c in range(2):
                crow = slice(c * CHUNK, (c + 1) * CHUNK)
                bg = _dot_tn(kd[u][crow], solb[u][crow])
                chunk = 2 * bi[u] + c
                grow = pl.ds(pl.multiple_of(chunk * GDN_HEAD, GDN_HEAD), GDN_HEAD)
                b_ref[h, grow, :] = bg[:, :GDN_HEAD]
                g_ref[h, grow, :] = cast(bg[:, GDN_HEAD:])
                dec_ref[h, pl.ds(chunk, 1), :] = glb[c * CHUNK:c * CHUNK + 1, :]
        return 0

    lax.fori_loop(0, ts // (sb * GDN_LOCAL_BLOCKS), local, 0)

    def scan(c, _):
        crow = pl.ds(pl.multiple_of(c * CHUNK, CHUNK), CHUNK)
        grow = pl.ds(pl.multiple_of(c * GDN_HEAD, GDN_HEAD), GDN_HEAD)
        state = [state_ref[h] for h in hs]
        sbf = [cast(state[h]) for h in hs]
        gs = [_dot(g_ref[h, grow, :], sbf[h]) for h in hs]
        for h in hs:
            state_ref[h] = state[h] * dec_ref[h, pl.ds(c, 1), :] - gs[h] + b_ref[h, grow, :]
        for h in hs:
            o_ref[h, crow, :] = _dot(qe_ref[h, crow, :], sbf[h]) + ol_ref[h, crow, :]
        return 0

    lax.fori_loop(0, ts // CHUNK, scan, 0)

    def norm(i, _):
        rows = pl.ds(pl.multiple_of(i * GDN_NORM_ROWS, GDN_NORM_ROWS), GDN_NORM_ROWS)
        for h in hs:
            o = o_ref[h, rows, :]
            o_ref[h, rows, :] = (o * lax.rsqrt(jnp.mean(o * o, axis=-1, keepdims=True) + RMS_EPS)
                                     * ng_ref[...])
        return 0

    lax.fori_loop(0, ts // GDN_NORM_ROWS, norm, 0)


def _gdn(q, k, v, p, pt, ng):
    bsz, _, s, _ = q.shape
    ts = min(GDN_TS, s)
    n_chunks = ts // CHUNK
    per_head = pl.BlockSpec((None, HEADS, ts, GDN_HEAD), lambda b, i: (b, 0, i, 0))
    return pl.pallas_call(
        _gdn_kernel,
        grid=(bsz, s // ts),
        in_specs=[per_head, per_head, per_head,
                  pl.BlockSpec((None, ts, SMALL_W), lambda b, i: (b, i, 0)),
                  pl.BlockSpec((None, 16, ts), lambda b, i: (b, 0, i)), _const_spec(ng.shape)],
        out_specs=per_head,
        out_shape=jax.ShapeDtypeStruct((bsz, HEADS, s, GDN_HEAD), f32),
        scratch_shapes=[pltpu.VMEM((HEADS, ts, GDN_HEAD), bf16), pltpu.VMEM((HEADS, ts, GDN_HEAD), f32),
                        pltpu.VMEM((HEADS, n_chunks * GDN_HEAD, GDN_HEAD), bf16),
                        pltpu.VMEM((HEADS, n_chunks * GDN_HEAD, GDN_HEAD), f32),
                        pltpu.VMEM((HEADS, n_chunks, GDN_HEAD), f32),
                        pltpu.VMEM((HEADS, GDN_HEAD, GDN_HEAD), f32)],
        compiler_params=_params(("parallel", "arbitrary")),
        name="gated_delta_rule",
    )(q, k, v, p, pt, ng)


def _merge_kernel(x_ref, mod_ref, oa_ref, ob_ref, on_ref, wgate_ref, wgg_ref,
                  wba_ref, wbb_ref, wbc_ref, wo_ref, lng_ref, lnb_ref, o_ref, h_ref):
    x = x_ref[...]
    d = x.shape[1]
    h_ref[...] = (x * (1.0 + mod_ref[1:2, :]) + mod_ref[0:1, :]).astype(bf16)
    gg = _dot(h_ref[...], wgg_ref[...])
    on = jnp.concatenate([on_ref[hd] for hd in range(HEADS)], axis=1)
    oc = (on * (gg * jax.nn.sigmoid(gg))).astype(bf16)
    branches = ((oa_ref[...], wba_ref), (ob_ref[...], wbb_ref), (oc, wbc_ref))
    merged = None
    for n, (o, wb_ref) in enumerate(branches):
        gate = jax.nn.sigmoid(_dot(h_ref[...], wgate_ref[:, n * d:(n + 1) * d]))
        term = gate * _dot(o, wb_ref[...])
        merged = term if merged is None else merged + term
    y = _dot(merged.astype(bf16), wo_ref[...])
    r = DEEPNORM_ALPHA * x + (1.0 + mod_ref[2:3, :]) * y
    o_ref[...] = _layer_norm(r, lng_ref[...], lnb_ref[...])


def _merge(x, mod, oa, ob, on, wgate, wgg, wba, wbb, wbc, wo, lng, lnb):
    bsz, s, d = x.shape
    tm = MERGE_TM
    row = lambda w: pl.BlockSpec((None, tm, w), lambda b, i: (b, i, 0))
    return pl.pallas_call(
        _merge_kernel,
        grid=(bsz, s // tm),
        in_specs=[row(d), pl.BlockSpec((None, 3, d), lambda b, i: (b, 0, 0)),
                  row(ATT_W), row(ATT_W),
                  pl.BlockSpec((None, HEADS, tm, GDN_HEAD), lambda b, i: (b, 0, i, 0)),
                  _const_spec(wgate.shape), _const_spec(wgg.shape), _const_spec(wba.shape),
                  _const_spec(wbb.shape), _const_spec(wbc.shape), _const_spec(wo.shape),
                  _const_spec((1, d)), _const_spec((1, d))],
        out_specs=row(d),
        out_shape=jax.ShapeDtypeStruct(x.shape, f32),
        scratch_shapes=[pltpu.VMEM((tm, d), bf16)],
        compiler_params=_params(("parallel", "parallel")),
        name="mixer_merge",
    )(x, mod, oa, ob, on, wgate, wgg, wba, wbb, wbc, wo, lng, lnb)


def _mixer_weights(w_in, b_forget, a_log, dt_bias):
    d = w_in.shape[0]
    sizes = (ATT_W, ATT_W, ATT_W, HEADS, ATT_W, ATT_W, ATT_W, 3 * GDN_W, HEADS, HEADS, GDN_W,
             3 * D_MODEL)
    offs = [0]
    for n in sizes:
        offs.append(offs[-1] + n)
    piece = lambda n: w_in[:, offs[n]:offs[n + 1]]
    wa = jnp.concatenate([piece(0), piece(1), piece(2)], axis=1).astype(bf16)
    wb = jnp.concatenate([piece(4), piece(5), piece(6)], axis=1).astype(bf16)
    wc = piece(7).astype(bf16)
    ws = jnp.concatenate([piece(3), piece(8), piece(9), piece(9),
                          jnp.zeros((d, SMALL_W - 4 * HEADS), w_in.dtype)], axis=1).astype(bf16)
    wgg = piece(10).astype(bf16)
    wgate = piece(11).astype(bf16)
    pad = lambda vec, at: jnp.zeros((SMALL_W,), f32).at[at:at + HEADS].set(vec.astype(f32))
    pv = jnp.stack([pad(b_forget, 0),
                    pad(a_log, 2 * HEADS) + pad(a_log, 3 * HEADS),
                    pad(dt_bias, 2 * HEADS) + pad(dt_bias, 3 * HEADS)]
                   + [jnp.zeros((SMALL_W,), f32)] * 5)
    return wa, wb, wc, ws, wgg, wgate, pv


def kernel(x, c, w_ada, b_ada, ln_g, ln_b, ffn_w_up, ffn_w_down, w_in, b_forget, conv_w,
           gdn_a_log, gdn_dt_bias, gdn_norm_g, w_branch, w_o):
    bsz = x.shape[0]
    mod = _ada(c, w_ada, b_ada).reshape(DEPTH, bsz, N_SUB, 3, D_MODEL)
    wup = ffn_w_up.astype(bf16)
    wdown = ffn_w_down.astype(bf16)
    for l in range(DEPTH):
        lng = lambda n: ln_g[l, n].reshape(1, D_MODEL)
        lnb = lambda n: ln_b[l, n].reshape(1, D_MODEL)
        x = _ffn(x, mod[l, :, 0], wup[l, 0], wdown[l, 0], lng(0), lnb(0))

        wa, wb, wc, ws, wgg, wgate, pv = _mixer_weights(w_in[l], b_forget[l], gdn_a_log[l],
                                                        gdn_dt_bias[l])
        m = mod[l, :, 1]
        fq, fkt, fv, sq, skt, sv, gq, gk, gv, sm = _proj(x, m, wa, wb, wc, ws, conv_w[l])
        p, pt = _prefix(sm, pv)
        oa = _fox(fq, fkt, fv, p, pt)
        ob = _sb(sq, skt, sv)
        on = _gdn(gq, gk, gv, p, pt, gdn_norm_g[l].reshape(1, GDN_HEAD))
        wbr = w_branch[l].astype(bf16)
        x = _merge(x, m, oa, ob, on, wgate, wgg, wbr[:ATT_W], wbr[ATT_W:2 * ATT_W],
                   wbr[2 * ATT_W:], w_o[l].astype(bf16), lng(1), lnb(1))

        x = _ffn(x, mod[l, :, 2], wup[l, 1], wdown[l, 1], lng(2), lnb(2))
    return x
```

```python
import functools

import jax
import jax.numpy as jnp
from jax import lax
from jax.experimental import pallas as pl
from jax.experimental.pallas import tpu as pltpu

f32 = jnp.float32
bf16 = jnp.bfloat16

D_MODEL = 1024
DEPTH = 2
N_SUB = 3
FFN_HIDDEN = 2816
HEADS = 4
ATT_W = 256
ATT_HEAD = 64
GDN_W = 512
GDN_HEAD = 128
CHUNK = 64
CONV_WIDTH = 4
SMALL_W = 128
DEEPNORM_ALPHA = (2.0 * DEPTH) ** 0.25
LN_EPS = 1e-5
RMS_EPS = 1e-6
LOG2E = 1.4426950408889634

FFN_TM = 512
FFN_HC = 256
PROJ_TM = 512
MERGE_TM = 512
LANES = 128
ATT_TQ = 256
ATT_TK = 256
FOX_TQ = 512
FOX_TK = 512
PREFIX_BLK = 256
GDN_SB = 128
GDN_TS = 1024
GDN_LOCAL_BLOCKS = 4
GDN_NORM_ROWS = 256
CONV_HALO = 8
NEG_BIG = -1e30
EXP2_UNDERFLOW = -151.0
VMEM_LIMIT = 56 * 1024 * 1024


def _dot(a, b):
    return jnp.dot(a, b, preferred_element_type=f32)


def _dot_nt(a, b):
    return lax.dot_general(a, b, (((1,), (1,)), ((), ())), preferred_element_type=f32)


def _dot_tn(a, b):
    return lax.dot_general(a, b, (((0,), (0,)), ((), ())), preferred_element_type=f32)


def _split2(x):
    hi = x.astype(bf16)
    lo = (x - hi.astype(f32)).astype(bf16)
    return hi, lo


def _split3(x):
    hi = x.astype(bf16)
    r = x - hi.astype(f32)
    mid = r.astype(bf16)
    lo = (r - mid.astype(f32)).astype(bf16)
    return hi, mid, lo


def _dot3(a, b):
    ah, al = _split2(a)
    bh, bl = _split2(b)
    return _dot(ah, bh) + (_dot(ah, bl) + _dot(al, bh))


def _layer_norm(r, g, b):
    mu = jnp.mean(r, axis=-1, keepdims=True)
    d = r - mu
    var = jnp.mean(d * d, axis=-1, keepdims=True)
    return d * lax.rsqrt(var + LN_EPS) * g + b


def _softplus_neg_abs(x):
    return jnp.log1p(jnp.exp(-jnp.abs(x)))


def _const_spec(shape):
    nd = len(shape)
    return pl.BlockSpec(shape, lambda *_: (0,) * nd, pipeline_mode=pl.Buffered(1))


def _params(sem):
    return pltpu.CompilerParams(dimension_semantics=sem, vmem_limit_bytes=VMEM_LIMIT)


def _ada_kernel(c_ref, w_ref, b_ref, o_ref):
    c = c_ref[...]
    sc = c * jax.nn.sigmoid(c)
    o_ref[...] = _dot3(sc, w_ref[...]) + b_ref[...]


def _ada(c, w_ada, b_ada):
    depth, d, n = w_ada.shape
    bsz = c.shape[0]
    tn = 1024
    return pl.pallas_call(
        _ada_kernel,
        grid=(depth, n // tn),
        in_specs=[pl.BlockSpec((bsz, d), lambda l, j: (0, 0)),
                  pl.BlockSpec((None, d, tn), lambda l, j: (l, 0, j)),
                  pl.BlockSpec((None, 1, tn), lambda l, j: (l, 0, j))],
        out_specs=pl.BlockSpec((None, bsz, tn), lambda l, j: (l, 0, j)),
        out_shape=jax.ShapeDtypeStruct((depth, bsz, n), f32),
        compiler_params=_params(("parallel", "parallel")),
        name="adaln",
    )(c, w_ada, b_ada.reshape(depth, 1, n))


def _ffn_kernel(x_ref, mod_ref, wup_ref, wd_ref, lng_ref, lnb_ref, o_ref, h_ref, acc_ref):
    x = x_ref[...]
    h_ref[...] = (x * (1.0 + mod_ref[1:2, :]) + mod_ref[0:1, :]).astype(bf16)
    for j in range(FFN_HIDDEN // FFN_HC):
        lo = j * FFN_HC
        h = h_ref[...]
        g = _dot(h, wup_ref[:, lo:lo + FFN_HC])
        u = _dot(h, wup_ref[:, FFN_HIDDEN + lo:FFN_HIDDEN + lo + FFN_HC])
        a = (g * jax.nn.sigmoid(g) * u).astype(bf16)
        y = _dot(a, wd_ref[lo:lo + FFN_HC, :])
        if j == 0:
            acc_ref[...] = y
        else:
            acc_ref[...] += y
    r = DEEPNORM_ALPHA * x + (0.5 * (1.0 + mod_ref[2:3, :])) * acc_ref[...]
    o_ref[...] = _layer_norm(r, lng_ref[...], lnb_ref[...])


def _ffn(x, mod, wup, wd, lng, lnb):
    bsz, s, d = x.shape
    tm = FFN_TM
    return pl.pallas_call(
        _ffn_kernel,
        grid=(bsz, s // tm),
        in_specs=[pl.BlockSpec((None, tm, d), lambda b, i: (b, i, 0)),
                  pl.BlockSpec((None, 3, d), lambda b, i: (b, 0, 0)),
                  _const_spec(wup.shape), _const_spec(wd.shape),
                  _const_spec((1, d)), _const_spec((1, d))],
        out_specs=pl.BlockSpec((None, tm, d), lambda b, i: (b, i, 0)),
        out_shape=jax.ShapeDtypeStruct(x.shape, f32),
        scratch_shapes=[pltpu.VMEM((tm, d), bf16), pltpu.VMEM((tm, d), f32)],
        compiler_params=_params(("parallel", "parallel")),
        name="ffn",
    )(x, mod, wup, wd, lng, lnb)


def _proj_kernel(x_ref, mod_ref, wa_ref, wb_ref, wc_ref, ws_ref, cw_ref,
                 fq_ref, fkt_ref, fv_ref, sq_ref, skt_ref, sv_ref,
                 gq_ref, gk_ref, gv_ref, sm_ref, h_ref, cbuf_ref):
    tm = x_ref.shape[0]
    h_ref[...] = (x_ref[...] * (1.0 + mod_ref[1:2, :]) + mod_ref[0:1, :]).astype(bf16)

    def attn_group(w_ref, q_ref, kt_ref, v_ref, q_scale):
        r = _dot(h_ref[...], w_ref[...])
        q_ref[...] = (r[:, :ATT_W] * q_scale).astype(bf16)
        kt_ref[...] = r[:, ATT_W:2 * ATT_W].T.astype(bf16)
        v_ref[...] = r[:, 2 * ATT_W:].astype(bf16)

    @pl.when(pl.program_id(1) == 0)
    def _():
        cbuf_ref[0:CONV_HALO, :] = jnp.zeros((CONV_HALO, cbuf_ref.shape[1]), f32)

    def conv_act(part):
        cols = slice(part * GDN_W, (part + 1) * GDN_W)
        cbuf_ref[CONV_HALO:, cols] = _dot(h_ref[...], wc_ref[:, cols])
        conv = cbuf_ref[CONV_HALO:, cols] * cw_ref[CONV_WIDTH - 1:CONV_WIDTH, cols]
        for tap in range(CONV_WIDTH - 1):
            back = CONV_WIDTH - 1 - tap
            conv = conv + (cbuf_ref[CONV_HALO - back:CONV_HALO - back + tm, cols]
                           * cw_ref[tap:tap + 1, cols])
        cbuf_ref[0:CONV_HALO, cols] = cbuf_ref[tm:tm + CONV_HALO, cols]
        return conv * jax.nn.sigmoid(conv)

    def l2norm_heads(act, o_ref, scale):
        for hd in range(HEADS):
            lanes = slice(hd * GDN_HEAD, (hd + 1) * GDN_HEAD)
            t = act[:, lanes]
            o_ref[hd] = t * lax.rsqrt(jnp.sum(t * t, -1, keepdims=True) + RMS_EPS) * scale

    l2norm_heads(conv_act(0), gq_ref, GDN_HEAD ** -0.5)
    attn_group(wa_ref, fq_ref, fkt_ref, fv_ref, LOG2E * ATT_HEAD ** -0.5)
    l2norm_heads(conv_act(1), gk_ref, 1.0)
    attn_group(wb_ref, sq_ref, skt_ref, sv_ref, LOG2E * ATT_HEAD ** -0.5)
    act_v = conv_act(2)
    for hd in range(HEADS):
        gv_ref[hd] = act_v[:, hd * GDN_HEAD:(hd + 1) * GDN_HEAD]
    sm_ref[...] = _dot(h_ref[...], ws_ref[...])


def _proj(x, mod, wa, wb, wc, ws, cw):
    bsz, s, d = x.shape
    tm = PROJ_TM
    row = lambda w: pl.BlockSpec((None, tm, w), lambda b, i: (b, i, 0))
    col = pl.BlockSpec((None, ATT_W, tm), lambda b, i: (b, 0, i))
    per_head = pl.BlockSpec((None, HEADS, tm, GDN_HEAD), lambda b, i: (b, 0, i, 0))
    sd = jax.ShapeDtypeStruct
    return pl.pallas_call(
        _proj_kernel,
        grid=(bsz, s // tm),
        in_specs=[row(d), pl.BlockSpec((None, 3, d), lambda b, i: (b, 0, 0)),
                  _const_spec(wa.shape), _const_spec(wb.shape), _const_spec(wc.shape),
                  _const_spec(ws.shape), _const_spec(cw.shape)],
        out_specs=[row(ATT_W), col, row(ATT_W), row(ATT_W), col, row(ATT_W),
                   per_head, per_head, per_head, row(SMALL_W)],
        out_shape=[sd((bsz, s, ATT_W), bf16), sd((bsz, ATT_W, s), bf16), sd((bsz, s, ATT_W), bf16),
                   sd((bsz, s, ATT_W), bf16), sd((bsz, ATT_W, s), bf16), sd((bsz, s, ATT_W), bf16),
                   sd((bsz, HEADS, s, GDN_HEAD), f32), sd((bsz, HEADS, s, GDN_HEAD), f32),
                   sd((bsz, HEADS, s, GDN_HEAD), f32), sd((bsz, s, SMALL_W), f32)],
        scratch_shapes=[pltpu.VMEM((tm, d), bf16), pltpu.VMEM((tm + CONV_HALO, 3 * GDN_W), f32)],
        compiler_params=_params(("parallel", "arbitrary")),
        name="mixer_in_proj",
    )(x, mod, wa, wb, wc, ws, cw)


def _prefix_kernel(sm_ref, pv_ref, p_ref, pt_ref):
    s = sm_ref.shape[0]
    blk = PREFIX_BLK
    ri = lax.broadcasted_iota(jnp.int32, (blk, blk), 0)
    ci = lax.broadcasted_iota(jnp.int32, (blk, blk), 1)
    low = ri >= ci
    same = (ri // CHUNK) == (ci // CHUNK)
    tri_full = jnp.where(low, 1.0, 0.0).astype(bf16)
    tri_chunk = jnp.where(low & same, 1.0, 0.0).astype(bf16)
    ones_chunk = jnp.where(same, 1.0, 0.0).astype(bf16)
    lane = lax.broadcasted_iota(jnp.int32, (blk, SMALL_W), 1)
    b_forget = pv_ref[0:1, :]
    neg_a = -jnp.exp(pv_ref[1:2, :])
    dt_bias = pv_ref[2:3, :]
    carry = jnp.zeros((1, SMALL_W), f32)
    for i in range(s // blk):
        x = sm_ref[i * blk:(i + 1) * blk, :]
        xf = x + b_forget
        log_f = jnp.minimum(xf, 0.0) - _softplus_neg_abs(xf)
        beta = jax.nn.sigmoid(x)
        xg = x + dt_bias
        g = neg_a * (jnp.maximum(xg, 0.0) + _softplus_neg_abs(xg))
        val = jnp.where(lane < 4, log_f, jnp.where(lane < 8, beta, jnp.where(lane < 16, g, 0.0)))
        parts = _split3(val)
        cum_full = sum(_dot(tri_full, p) for p in parts) + carry
        cum_chunk = sum(_dot(tri_chunk, p) for p in parts)
        tot_chunk = sum(_dot(ones_chunk, p) for p in parts)
        carry = cum_full[blk - 1:blk, :]
        out = jnp.where(lane < 4, cum_full,
                        jnp.where(lane < 8, val, jnp.where(lane < 12, cum_chunk, tot_chunk)))
        p_ref[i * blk:(i + 1) * blk, :] = out
        pt_ref[:, i * blk:(i + 1) * blk] = out.T[0:16, :]


def _prefix(sm, pv):
    bsz, s, _ = sm.shape
    return pl.pallas_call(
        _prefix_kernel,
        grid=(bsz,),
        in_specs=[pl.BlockSpec((None, s, SMALL_W), lambda b: (b, 0, 0)), _const_spec(pv.shape)],
        out_specs=[pl.BlockSpec((None, s, SMALL_W), lambda b: (b, 0, 0)),
                   pl.BlockSpec((None, 16, s), lambda b: (b, 0, 0))],
        out_shape=[jax.ShapeDtypeStruct((bsz, s, SMALL_W), f32),
                   jax.ShapeDtypeStruct((bsz, 16, s), f32)],
        compiler_params=_params(("parallel",)),
        name="gate_prefix",
    )(sm, pv)


def _head_masks(shape, axis):
    lane = lax.broadcasted_iota(jnp.int32, shape, axis)
    return [(lane >= hd * ATT_HEAD) & (lane < (hd + 1) * ATT_HEAD) for hd in range(HEADS)]


def _by_head(masks, cols):
    out = cols[HEADS - 1]
    for hd in range(HEADS - 2, -1, -1):
        out = jnp.where(masks[hd], cols[hd], out)
    return out


def _fox_kernel(q_ref, kt_ref, v_ref, p_ref, pt_ref, o_ref,
                vm_ref, s_ref, cq_ref, m_ref, l_ref, acc_ref):
    tq, tk, lw = FOX_TQ, FOX_TK, LANES
    hs = range(HEADS)
    groups = range(tk // lw)
    i = pl.program_id(1)
    q = q_ref[...]
    qmask = _head_masks((tq, ATT_W), 1)
    zero = jnp.zeros((), bf16)
    qm = [jnp.where(qmask[hd], q, zero) for hd in hs]

    @pl.when(i == 0)
    def _():
        vmask = _head_masks(v_ref.shape, 1)
        v = v_ref[...]
        for hd in hs:
            vm_ref[hd] = jnp.where(vmask[hd], v, zero)

    delta = (lax.broadcasted_iota(jnp.int32, (tq, lw), 1)
             - lax.broadcasted_iota(jnp.int32, (tq, lw), 0))
    n_full = (i * tq) // tk

    def score_block(j, _, masked=False):
        off = pl.multiple_of(j * tk, tk)
        kt = kt_ref[:, pl.ds(off, tk)]
        for hd in hs:
            s = _dot(qm[hd], kt)
            cq = cq_ref[hd]
            ck = pt_ref[hd:hd + 1, pl.ds(off, tk)] * LOG2E
            mx = m_ref[hd]
            for g in groups:
                sg = s[:, g * lw:(g + 1) * lw] + cq - ck[:, g * lw:(g + 1) * lw]
                if masked:
                    sg = jnp.where(delta <= i * tq - off - g * lw, sg, NEG_BIG)
                s_ref[hd, j * len(groups) + g] = sg
                mx = jnp.maximum(mx, sg)
            m_ref[hd] = mx
        return 0

    m_ref[...] = jnp.full(m_ref.shape, NEG_BIG, f32)
    for hd in hs:
        cq_ref[hd] = jnp.broadcast_to(p_ref[:, hd:hd + 1], (tq, lw)) * LOG2E
    lax.fori_loop(0, n_full, score_block, 0)
    score_block(n_full, 0, masked=True)
    for hd in hs:
        m_ref[hd] = jnp.broadcast_to(jnp.max(m_ref[hd], axis=-1, keepdims=True), (tq, lw))

    def sum_block(j, _):
        off = pl.multiple_of(j * tk, tk)
        pv = None
        for hd in hs:
            mx = m_ref[hd]
            parts = [jnp.exp2(s_ref[hd, j * len(groups) + g] - mx) for g in groups]
            l = l_ref[hd]
            for g in groups:
                l = l + parts[g]
            l_ref[hd] = l
            p = jnp.concatenate(parts, axis=1)
            c = _dot(p.astype(bf16), vm_ref[hd, pl.ds(off, tk), :])
            pv = c if pv is None else pv + c
        acc_ref[...] += pv
        return 0

    l_ref[...] = jnp.zeros(l_ref.shape, f32)
    acc_ref[...] = jnp.zeros(acc_ref.shape, f32)
    lax.fori_loop(0, n_full + 1, sum_block, 0)
    l = [jnp.sum(l_ref[hd], axis=-1, keepdims=True) for hd in hs]
    o_ref[...] = (acc_ref[...] / _by_head(qmask, l)).astype(bf16)


def _fox(q, kt, v, p, pt):
    bsz, s, _ = q.shape
    tq = FOX_TQ
    return pl.pallas_call(
        _fox_kernel,
        grid=(bsz, s // tq),
        in_specs=[pl.BlockSpec((None, tq, ATT_W), lambda b, i: (b, i, 0)),
                  pl.BlockSpec((None, ATT_W, s), lambda b, i: (b, 0, 0)),
                  pl.BlockSpec((None, s, ATT_W), lambda b, i: (b, 0, 0)),
                  pl.BlockSpec((None, tq, SMALL_W), lambda b, i: (b, i, 0)),
                  pl.BlockSpec((None, 16, s), lambda b, i: (b, 0, 0))],
        out_specs=pl.BlockSpec((None, tq, ATT_W), lambda b, i: (b, i, 0)),
        out_shape=jax.ShapeDtypeStruct((bsz, s, ATT_W), bf16),
        scratch_shapes=[pltpu.VMEM((HEADS, s, ATT_W), bf16),
                        pltpu.VMEM((HEADS, s // LANES, tq, LANES), f32),
                        pltpu.VMEM((HEADS, tq, LANES), f32),
                        pltpu.VMEM((HEADS, tq, LANES), f32),
                        pltpu.VMEM((HEADS, tq, LANES), f32),
                        pltpu.VMEM((tq, ATT_W), f32)],
        compiler_params=_params(("parallel", "arbitrary")),
        name="forgetting_attention",
    )(q, kt, v, p, pt)


def _sb_kernel(q_ref, kt_ref, v_ref, o_ref):
    tq, tk = ATT_TQ, ATT_TK
    i = pl.program_id(1)
    q = q_ref[...]
    qmask = _head_masks((tq, ATT_W), 1)
    vmask = _head_masks((tk, ATT_W), 1)
    zero = jnp.zeros((), bf16)
    qm = [jnp.where(qmask[hd], q, zero) for hd in range(HEADS)]
    rows = lax.broadcasted_iota(jnp.int32, (tq, tk), 0)
    cols = lax.broadcasted_iota(jnp.int32, (tq, tk), 1)
    strict = cols < rows
    kr = lax.broadcasted_iota(jnp.int32, (tk, tk), 0)
    kc = lax.broadcasted_iota(jnp.int32, (tk, tk), 1)
    later = jnp.where(kr > kc, 1.0, 0.0).astype(bf16)

    def block(j, rem, acc, diag):
        off = pl.multiple_of(j * tk, tk)
        kt = kt_ref[:, pl.ds(off, tk)]
        v = v_ref[pl.ds(off, tk), :]
        hs = range(HEADS)
        z = [_dot(qm[hd], kt) for hd in hs]
        log_beta = [jnp.minimum(z[hd], 0.0) - jnp.log2(1.0 + jnp.exp2(-jnp.abs(z[hd])))
                    for hd in hs]
        log_rem = [log_beta[hd] - z[hd] for hd in hs]
        if diag:
            log_rem = [jnp.where(strict, log_rem[hd], 0.0) for hd in hs]
        parts = [_split2(log_rem[hd]) for hd in hs]
        after = [_dot(parts[hd][0], later) + _dot(parts[hd][1], later) + rem[hd] for hd in hs]
        w = [jnp.exp2(log_beta[hd] + after[hd]) for hd in hs]
        if diag:
            w = [jnp.where(strict, w[hd], 0.0) for hd in hs]
        rem_new = [rem[hd] + jnp.sum(log_rem[hd], axis=-1, keepdims=True) for hd in hs]
        for hd in hs:
            acc = acc + _dot(w[hd].astype(bf16), jnp.where(vmask[hd], v, zero))
        return tuple(rem_new), acc

    def first_blocks(n_blocks):
        rem, acc = block(i, tuple(jnp.zeros((tq, 1), f32) for _ in range(HEADS)),
                         jnp.zeros((tq, ATT_W), f32), diag=True)
        if n_blocks == 2:
            rem, acc = block(i - 1, rem, acc, diag=False)
        return rem, acc, i - n_blocks

    rem, acc, j_next = lax.cond(i > 0, lambda: first_blocks(2), lambda: first_blocks(1))

    def live(rem):
        top = rem[0]
        for hd in range(1, HEADS):
            top = jnp.maximum(top, rem[hd])
        return jnp.max(top) > EXP2_UNDERFLOW

    def cond(state):
        return jnp.logical_and(state[0] >= 0, state[1])

    def body(state):
        j, _, rem, acc = state
        rem, acc = block(j, rem, acc, diag=False)
        return j - 1, live(rem), rem, acc

    _, _, _, acc = lax.while_loop(cond, body, (j_next, live(rem), rem, acc))
    o_ref[...] = acc.astype(bf16)


def _sb(q, kt, v):
    bsz, s, _ = q.shape
    tq = ATT_TQ
    return pl.pallas_call(
        _sb_kernel,
        grid=(bsz, s // tq),
        in_specs=[pl.BlockSpec((None, tq, ATT_W), lambda b, i: (b, i, 0)),
                  pl.BlockSpec((None, ATT_W, s), lambda b, i: (b, 0, 0)),
                  pl.BlockSpec((None, s, ATT_W), lambda b, i: (b, 0, 0))],
        out_specs=pl.BlockSpec((None, tq, ATT_W), lambda b, i: (b, i, 0)),
        out_shape=jax.ShapeDtypeStruct((bsz, s, ATT_W), bf16),
        compiler_params=_params(("parallel", "parallel")),
        name="stick_breaking_attention",
    )(q, kt, v)


def _gdn_kernel(q_ref, k_ref, v_ref, p_ref, pt_ref, ng_ref, o_ref,
                qe_ref, ol_ref, g_ref, b_ref, dec_ref, state_ref):
    ts = q_ref.shape[1]
    sb = GDN_SB
    hs = range(HEADS)
    lanes = [slice(hd * GDN_HEAD, (hd + 1) * GDN_HEAD) for hd in hs]
    ri = lax.broadcasted_iota(jnp.int32, (sb, sb), 0)
    ci = lax.broadcasted_iota(jnp.int32, (sb, sb), 1)
    same = (ri // CHUNK) == (ci // CHUNK)
    incl = same & (ri >= ci)
    strict = same & (ri > ci)
    cast = lambda t: t.astype(bf16)

    @pl.when(pl.program_id(1) == 0)
    def _():
        state_ref[...] = jnp.zeros(state_ref.shape, f32)

    def local(i, _):
        units = [(blk, h) for blk in range(GDN_LOCAL_BLOCKS) for h in hs]
        us = range(len(units))
        bi = [i * GDN_LOCAL_BLOCKS + blk for blk, _ in units]
        rows = [pl.ds(pl.multiple_of(bi[u] * sb, sb), sb) for u in us]
        hd = [h for _, h in units]
        k = [k_ref[hd[u], rows[u], :] for u in us]
        beta = [p_ref[rows[u], 4 + hd[u]:5 + hd[u]] for u in us]
        gc = [p_ref[rows[u], 8 + hd[u]:9 + hd[u]] for u in us]
        gl = [p_ref[rows[u], 12 + hd[u]:13 + hd[u]] for u in us]
        decay = [jnp.where(incl, jnp.exp(jnp.where(
            incl, gc[u] - pt_ref[8 + hd[u]:9 + hd[u], rows[u]], 0.0)), 0.0) for u in us]
        kb = [k[u] * beta[u] for u in us]
        kbf = [cast(k[u]) for u in us]
        a = [jnp.where(strict, _dot_nt(cast(kb[u]), kbf[u]) * decay[u], 0.0) for u in us]
        ab = [cast(a[u]) for u in us]
        pw = [_dot(ab[u], ab[u]) for u in us]
        n = [-a[u] for u in us]
        for level in range(5):
            pb = [cast(pw[u]) for u in us]
            n = [n[u] + pw[u] + _dot(cast(n[u]), pb[u]) for u in us]
            if level < 4:
                pw = [_dot(pb[u], pb[u]) for u in us]
        e_gc = [jnp.exp(gc[u]) for u in us]
        rhs = [jnp.concatenate([v_ref[hd[u], rows[u], :] * beta[u], kb[u] * e_gc[u]], axis=1)
               for u in us]
        sol = [rhs[u] + _dot(cast(n[u]), cast(rhs[u])) for u in us]
        solb = [cast(sol[u]) for u in us]
        q = [q_ref[hd[u], rows[u], :] for u in us]
        intra = [cast(jnp.where(incl, _dot_nt(cast(q[u]), kbf[u]) * decay[u], 0.0)) for u in us]
        prod = [_dot(intra[u], solb[u]) for u in us]
        kd = [cast(k[u] * jnp.exp(gl[u] - gc[u])) for u in us]
        for u in us:
            h = hd[u]
            ol_ref[h, rows[u], :] = prod[u][:, :GDN_HEAD]
            qe_ref[h, rows[u], :] = cast(q[u] * e_gc[u] - prod[u][:, GDN_HEAD:])
            glb = jnp.exp(jnp.broadcast_to(gl[u], (sb, GDN_HEAD)))
            for c in range(2):
                crow = slice(c * CHUNK, (c + 1) * CHUNK)
                bg = _dot_tn(kd[u][crow], solb[u][crow])
                chunk = 2 * bi[u] + c
                grow = pl.ds(pl.multiple_of(chunk * GDN_HEAD, GDN_HEAD), GDN_HEAD)
                b_ref[h, grow, :] = bg[:, :GDN_HEAD]
                g_ref[h, grow, :] = cast(bg[:, GDN_HEAD:])
                dec_ref[h, pl.ds(chunk, 1), :] = glb[c * CHUNK:c * CHUNK + 1, :]
        return 0

    lax.fori_loop(0, ts // (sb * GDN_LOCAL_BLOCKS), local, 0)

    def scan(c, _):
        crow = pl.ds(pl.multiple_of(c * CHUNK, CHUNK), CHUNK)
        grow = pl.ds(pl.multiple_of(c * GDN_HEAD, GDN_HEAD), GDN_HEAD)
        state = [state_ref[h] for h in hs]
        sbf = [cast(state[h]) for h in hs]
        gs = [_dot(g_ref[h, grow, :], sbf[h]) for h in hs]
        for h in hs:
            state_ref[h] = state[h] * dec_ref[h, pl.ds(c, 1), :] - gs[h] + b_ref[h, grow, :]
        for h in hs:
            o_ref[h, crow, :] = _dot(qe_ref[h, crow, :], sbf[h]) + ol_ref[h, crow, :]
        return 0

    lax.fori_loop(0, ts // CHUNK, scan, 0)

    def norm(i, _):
        rows = pl.ds(pl.multiple_of(i * GDN_NORM_ROWS, GDN_NORM_ROWS), GDN_NORM_ROWS)
        for h in hs:
            o = o_ref[h, rows, :]
            o_ref[h, rows, :] = (o * lax.rsqrt(jnp.mean(o * o, axis=-1, keepdims=True) + RMS_EPS)
                                     * ng_ref[...])
        return 0

    lax.fori_loop(0, ts // GDN_NORM_ROWS, norm, 0)


def _gdn(q, k, v, p, pt, ng):
    bsz, _, s, _ = q.shape
    ts = min(GDN_TS, s)
    n_chunks = ts // CHUNK
    per_head = pl.BlockSpec((None, HEADS, ts, GDN_HEAD), lambda b, i: (b, 0, i, 0))
    return pl.pallas_call(
        _gdn_kernel,
        grid=(bsz, s // ts),
        in_specs=[per_head, per_head, per_head,
                  pl.BlockSpec((None, ts, SMALL_W), lambda b, i: (b, i, 0)),
                  pl.BlockSpec((None, 16, ts), lambda b, i: (b, 0, i)), _const_spec(ng.shape)],
        out_specs=per_head,
        out_shape=jax.ShapeDtypeStruct((bsz, HEADS, s, GDN_HEAD), f32),
        scratch_shapes=[pltpu.VMEM((HEADS, ts, GDN_HEAD), bf16), pltpu.VMEM((HEADS, ts, GDN_HEAD), f32),
                        pltpu.VMEM((HEADS, n_chunks * GDN_HEAD, GDN_HEAD), bf16),
                        pltpu.VMEM((HEADS, n_chunks * GDN_HEAD, GDN_HEAD), f32),
                        pltpu.VMEM((HEADS, n_chunks, GDN_HEAD), f32),
                        pltpu.VMEM((HEADS, GDN_HEAD, GDN_HEAD), f32)],
        compiler_params=_params(("parallel", "arbitrary")),
        name="gated_delta_rule",
    )(q, k, v, p, pt, ng)


def _merge_kernel(x_ref, mod_ref, oa_ref, ob_ref, on_ref, wgate_ref, wgg_ref,
                  wba_ref, wbb_ref, wbc_ref, wo_ref, lng_ref, lnb_ref, o_ref, h_ref):
    x = x_ref[...]
    d = x.shape[1]
    h_ref[...] = (x * (1.0 + mod_ref[1:2, :]) + mod_ref[0:1, :]).astype(bf16)
    gg = _dot(h_ref[...], wgg_ref[...])
    on = jnp.concatenate([on_ref[hd] for hd in range(HEADS)], axis=1)
    oc = (on * (gg * jax.nn.sigmoid(gg))).astype(bf16)
    branches = ((oa_ref[...], wba_ref), (ob_ref[...], wbb_ref), (oc, wbc_ref))
    merged = None
    for n, (o, wb_ref) in enumerate(branches):
        gate = jax.nn.sigmoid(_dot(h_ref[...], wgate_ref[:, n * d:(n + 1) * d]))
        term = gate * _dot(o, wb_ref[...])
        merged = term if merged is None else merged + term
    y = _dot(merged.astype(bf16), wo_ref[...])
    r = DEEPNORM_ALPHA * x + (1.0 + mod_ref[2:3, :]) * y
    o_ref[...] = _layer_norm(r, lng_ref[...], lnb_ref[...])


def _merge(x, mod, oa, ob, on, wgate, wgg, wba, wbb, wbc, wo, lng, lnb):
    bsz, s, d = x.shape
    tm = MERGE_TM
    row = lambda w: pl.BlockSpec((None, tm, w), lambda b, i: (b, i, 0))
    return pl.pallas_call(
        _merge_kernel,
        grid=(bsz, s // tm),
        in_specs=[row(d), pl.BlockSpec((None, 3, d), lambda b, i: (b, 0, 0)),
                  row(ATT_W), row(ATT_W),
                  pl.BlockSpec((None, HEADS, tm, GDN_HEAD), lambda b, i: (b, 0, i, 0)),
                  _const_spec(wgate.shape), _const_spec(wgg.shape), _const_spec(wba.shape),
                  _const_spec(wbb.shape), _const_spec(wbc.shape), _const_spec(wo.shape),
                  _const_spec((1, d)), _const_spec((1, d))],
        out_specs=row(d),
        out_shape=jax.ShapeDtypeStruct(x.shape, f32),
        scratch_shapes=[pltpu.VMEM((tm, d), bf16)],
        compiler_params=_params(("parallel", "parallel")),
        name="mixer_merge",
    )(x, mod, oa, ob, on, wgate, wgg, wba, wbb, wbc, wo, lng, lnb)


def _mixer_weights(w_in, b_forget, a_log, dt_bias):
    d = w_in.shape[0]
    sizes = (ATT_W, ATT_W, ATT_W, HEADS, ATT_W, ATT_W, ATT_W, 3 * GDN_W, HEADS, HEADS, GDN_W,
             3 * D_MODEL)
    offs = [0]
    for n in sizes:
        offs.append(offs[-1] + n)
    piece = lambda n: w_in[:, offs[n]:offs[n + 1]]
    wa = jnp.concatenate([piece(0), piece(1), piece(2)], axis=1).astype(bf16)
    wb = jnp.concatenate([piece(4), piece(5), piece(6)], axis=1).astype(bf16)
    wc = piece(7).astype(bf16)
    ws = jnp.concatenate([piece(3), piece(8), piece(9), piece(9),
                          jnp.zeros((d, SMALL_W - 4 * HEADS), w_in.dtype)], axis=1).astype(bf16)
    wgg = piece(10).astype(bf16)
    wgate = piece(11).astype(bf16)
    pad = lambda vec, at: jnp.zeros((SMALL_W,), f32).at[at:at + HEADS].set(vec.astype(f32))
    pv = jnp.stack([pad(b_forget, 0),
                    pad(a_log, 2 * HEADS) + pad(a_log, 3 * HEADS),
                    pad(dt_bias, 2 * HEADS) + pad(dt_bias, 3 * HEADS)]
                   + [jnp.zeros((SMALL_W,), f32)] * 5)
    return wa, wb, wc, ws, wgg, wgate, pv


def kernel(x, c, w_ada, b_ada, ln_g, ln_b, ffn_w_up, ffn_w_down, w_in, b_forget, conv_w,
           gdn_a_log, gdn_dt_bias, gdn_norm_g, w_branch, w_o):
    bsz = x.shape[0]
    mod = _ada(c, w_ada, b_ada).reshape(DEPTH, bsz, N_SUB, 3, D_MODEL)
    wup = ffn_w_up.astype(bf16)
    wdown = ffn_w_down.astype(bf16)
    for l in range(DEPTH):
        lng = lambda n: ln_g[l, n].reshape(1, D_MODEL)
        lnb = lambda n: ln_b[l, n].reshape(1, D_MODEL)
        x = _ffn(x, mod[l, :, 0], wup[l, 0], wdown[l, 0], lng(0), lnb(0))

        wa, wb, wc, ws, wgg, wgate, pv = _mixer_weights(w_in[l], b_forget[l], gdn_a_log[l],
                                                        gdn_dt_bias[l])
        m = mod[l, :, 1]
        fq, fkt, fv, sq, skt, sv, gq, gk, gv, sm = _proj(x, m, wa, wb, wc, ws, conv_w[l])
        p, pt = _prefix(sm, pv)
        oa = _fox(fq, fkt, fv, p, pt)
        ob = _sb(sq, skt, sv)
        on = _gdn(gq, gk, gv, p, pt, gdn_norm_g[l].reshape(1, GDN_HEAD))
        wbr = w_branch[l].astype(bf16)
        x = _merge(x, m, oa, ob, on, wgate, wgg, wbr[:ATT_W], wbr[ATT_W:2 * ATT_W],
                   wbr[2 * ATT_W:], w_o[l].astype(bf16), lng(1), lnb(1))

        x = _ffn(x, mod[l, :, 2], wup[l, 1], wdown[l, 1], lng(2), lnb(2))
    return x
```

```python
import functools

import jax
import jax.numpy as jnp
from jax import lax
from jax.experimental import pallas as pl
from jax.experimental.pallas import tpu as pltpu

f32 = jnp.float32
bf16 = jnp.bfloat16

D_MODEL = 1024
DEPTH = 2
N_SUB = 3
FFN_HIDDEN = 2816
HEADS = 4
ATT_W = 256
ATT_HEAD = 64
GDN_W = 512
GDN_HEAD = 128
CHUNK = 64
CONV_WIDTH = 4
SMALL_W = 128
SMALL_FEATS = 16
DEEPNORM_ALPHA = (2.0 * DEPTH) ** 0.25
LN_EPS = 1e-5
RMS_EPS = 1e-6
LOG2E = 1.4426950408889634

FFN_TM = 512
FFN_HC = 256
PROJ_TM = 512
MERGE_TM = 512
LANES = 128
ATT_TQ = 256
ATT_TK = 256
FOX_TQ = 512
FOX_TK = 512
PREFIX_BLK = 256
GDN_SB = 128
GDN_TS = 1024
GDN_LOCAL_BLOCKS = 4
GDN_NORM_ROWS = 256
CONV_HALO = 8
NEG_BIG = -1e30
EXP2_UNDERFLOW = -151.0
VMEM_LIMIT = 56 * 1024 * 1024


def _dot(a, b):
    return jnp.dot(a, b, preferred_element_type=f32)


def _dot_nt(a, b):
    return lax.dot_general(a, b, (((1,), (1,)), ((), ())), preferred_element_type=f32)


def _dot_tn(a, b):
    return lax.dot_general(a, b, (((0,), (0,)), ((), ())), preferred_element_type=f32)


def _split2(x):
    hi = x.astype(bf16)
    lo = (x - hi.astype(f32)).astype(bf16)
    return hi, lo


def _split3(x):
    hi = x.astype(bf16)
    r = x - hi.astype(f32)
    mid = r.astype(bf16)
    lo = (r - mid.astype(f32)).astype(bf16)
    return hi, mid, lo


def _dot3(a, b):
    ah, al = _split2(a)
    bh, bl = _split2(b)
    return _dot(ah, bh) + (_dot(ah, bl) + _dot(al, bh))


def _layer_norm(r, g, b):
    mu = jnp.mean(r, axis=-1, keepdims=True)
    d = r - mu
    var = jnp.mean(d * d, axis=-1, keepdims=True)
    return d * lax.rsqrt(var + LN_EPS) * g + b


def _softplus_neg_abs(x):
    return jnp.log1p(jnp.exp(-jnp.abs(x)))


def _const_spec(shape):
    nd = len(shape)
    return pl.BlockSpec(shape, lambda *_: (0,) * nd, pipeline_mode=pl.Buffered(1))


def _params(sem):
    return pltpu.CompilerParams(dimension_semantics=sem, vmem_limit_bytes=VMEM_LIMIT)


def _ada_kernel(c_ref, w_ref, b_ref, o_ref):
    c = c_ref[...]
    sc = c * jax.nn.sigmoid(c)
    o_ref[...] = _dot3(sc, w_ref[...]) + b_ref[...]


def _ada(c, w_ada, b_ada):
    depth, d, n = w_ada.shape
    bsz = c.shape[0]
    tn = 1024
    return pl.pallas_call(
        _ada_kernel,
        grid=(depth, n // tn),
        in_specs=[pl.BlockSpec((bsz, d), lambda l, j: (0, 0)),
                  pl.BlockSpec((None, d, tn), lambda l, j: (l, 0, j)),
                  pl.BlockSpec((None, 1, tn), lambda l, j: (l, 0, j))],
        out_specs=pl.BlockSpec((None, bsz, tn), lambda l, j: (l, 0, j)),
        out_shape=jax.ShapeDtypeStruct((depth, bsz, n), f32),
        compiler_params=_params(("parallel", "parallel")),
        name="adaln",
    )(c, w_ada, b_ada.reshape(depth, 1, n))


def _ffn_kernel(x_ref, mod_ref, wup_ref, wd_ref, lng_ref, lnb_ref, o_ref, h_ref, acc_ref):
    x = x_ref[...]
    h_ref[...] = (x * (1.0 + mod_ref[1:2, :]) + mod_ref[0:1, :]).astype(bf16)
    for j in range(FFN_HIDDEN // FFN_HC):
        lo = j * FFN_HC
        h = h_ref[...]
        g = _dot(h, wup_ref[:, lo:lo + FFN_HC])
        u = _dot(h, wup_ref[:, FFN_HIDDEN + lo:FFN_HIDDEN + lo + FFN_HC])
        a = (g * jax.nn.sigmoid(g) * u).astype(bf16)
        y = _dot(a, wd_ref[lo:lo + FFN_HC, :])
        if j == 0:
            acc_ref[...] = y
        else:
            acc_ref[...] += y
    r = DEEPNORM_ALPHA * x + (0.5 * (1.0 + mod_ref[2:3, :])) * acc_ref[...]
    o_ref[...] = _layer_norm(r, lng_ref[...], lnb_ref[...])


def _ffn(x, mod, wup, wd, lng, lnb):
    bsz, s, d = x.shape
    tm = FFN_TM
    return pl.pallas_call(
        _ffn_kernel,
        grid=(bsz, s // tm),
        in_specs=[pl.BlockSpec((None, tm, d), lambda b, i: (b, i, 0)),
                  pl.BlockSpec((None, 3, d), lambda b, i: (b, 0, 0)),
                  _const_spec(wup.shape), _const_spec(wd.shape),
                  _const_spec((1, d)), _const_spec((1, d))],
        out_specs=pl.BlockSpec((None, tm, d), lambda b, i: (b, i, 0)),
        out_shape=jax.ShapeDtypeStruct(x.shape, f32),
        scratch_shapes=[pltpu.VMEM((tm, d), bf16), pltpu.VMEM((tm, d), f32)],
        compiler_params=_params(("parallel", "parallel")),
        name="ffn",
    )(x, mod, wup, wd, lng, lnb)


def _proj_kernel(x_ref, mod_ref, wa_ref, wb_ref, wc_ref, ws_ref, cw_ref,
                 fq_ref, fkt_ref, fv_ref, sq_ref, skt_ref, sv_ref,
                 gq_ref, gk_ref, gv_ref, sm_ref, h_ref, cbuf_ref):
    tm = x_ref.shape[0]
    h_ref[...] = (x_ref[...] * (1.0 + mod_ref[1:2, :]) + mod_ref[0:1, :]).astype(bf16)

    def attn_group(w_ref, q_ref, kt_ref, v_ref, q_scale):
        r = _dot(h_ref[...], w_ref[...])
        q_ref[...] = (r[:, :ATT_W] * q_scale).astype(bf16)
        kt_ref[...] = r[:, ATT_W:2 * ATT_W].T.astype(bf16)
        v_ref[...] = r[:, 2 * ATT_W:].astype(bf16)

    @pl.when(pl.program_id(1) == 0)
    def _():
        cbuf_ref[0:CONV_HALO, :] = jnp.zeros((CONV_HALO, cbuf_ref.shape[1]), f32)

    def conv_act(part):
        cols = slice(part * GDN_W, (part + 1) * GDN_W)
        cbuf_ref[CONV_HALO:, cols] = _dot(h_ref[...], wc_ref[:, cols])
        conv = cbuf_ref[CONV_HALO:, cols] * cw_ref[CONV_WIDTH - 1:CONV_WIDTH, cols]
        for tap in range(CONV_WIDTH - 1):
            back = CONV_WIDTH - 1 - tap
            conv = conv + (cbuf_ref[CONV_HALO - back:CONV_HALO - back + tm, cols]
                           * cw_ref[tap:tap + 1, cols])
        cbuf_ref[0:CONV_HALO, cols] = cbuf_ref[tm:tm + CONV_HALO, cols]
        return conv * jax.nn.sigmoid(conv)

    def l2norm_heads(act, o_ref, scale):
        for hd in range(HEADS):
            lanes = slice(hd * GDN_HEAD, (hd + 1) * GDN_HEAD)
            t = act[:, lanes]
            o_ref[hd] = t * lax.rsqrt(jnp.sum(t * t, -1, keepdims=True) + RMS_EPS) * scale

    l2norm_heads(conv_act(0), gq_ref, GDN_HEAD ** -0.5)
    attn_group(wa_ref, fq_ref, fkt_ref, fv_ref, LOG2E * ATT_HEAD ** -0.5)
    l2norm_heads(conv_act(1), gk_ref, 1.0)
    attn_group(wb_ref, sq_ref, skt_ref, sv_ref, LOG2E * ATT_HEAD ** -0.5)
    act_v = conv_act(2)
    for hd in range(HEADS):
        gv_ref[hd] = act_v[:, hd * GDN_HEAD:(hd + 1) * GDN_HEAD]
    sm_ref[...] = _dot(h_ref[...], ws_ref[...])


def _proj(x, mod, wa, wb, wc, ws, cw):
    bsz, s, d = x.shape
    tm = PROJ_TM
    row = lambda w: pl.BlockSpec((None, tm, w), lambda b, i: (b, i, 0))
    col = pl.BlockSpec((None, ATT_W, tm), lambda b, i: (b, 0, i))
    per_head = pl.BlockSpec((None, HEADS, tm, GDN_HEAD), lambda b, i: (b, 0, i, 0))
    sd = jax.ShapeDtypeStruct
    return pl.pallas_call(
        _proj_kernel,
        grid=(bsz, s // tm),
        in_specs=[row(d), pl.BlockSpec((None, 3, d), lambda b, i: (b, 0, 0)),
                  _const_spec(wa.shape), _const_spec(wb.shape), _const_spec(wc.shape),
                  _const_spec(ws.shape), _const_spec(cw.shape)],
        out_specs=[row(ATT_W), col, row(ATT_W), row(ATT_W), col, row(ATT_W),
                   per_head, per_head, per_head, row(SMALL_W)],
        out_shape=[sd((bsz, s, ATT_W), bf16), sd((bsz, ATT_W, s), bf16), sd((bsz, s, ATT_W), bf16),
                   sd((bsz, s, ATT_W), bf16), sd((bsz, ATT_W, s), bf16), sd((bsz, s, ATT_W), bf16),
                   sd((bsz, HEADS, s, GDN_HEAD), f32), sd((bsz, HEADS, s, GDN_HEAD), f32),
                   sd((bsz, HEADS, s, GDN_HEAD), f32), sd((bsz, s, SMALL_W), f32)],
        scratch_shapes=[pltpu.VMEM((tm, d), bf16), pltpu.VMEM((tm + CONV_HALO, 3 * GDN_W), f32)],
        compiler_params=_params(("parallel", "arbitrary")),
        name="mixer_in_proj",
    )(x, mod, wa, wb, wc, ws, cw)


def _prefix_kernel(sm_ref, pv_ref, p_ref, pt_ref):
    s = sm_ref.shape[0]
    blk, nf = PREFIX_BLK, SMALL_FEATS
    src = lax.broadcasted_iota(jnp.int32, (blk, blk), 0)
    dst = lax.broadcasted_iota(jnp.int32, (blk, blk), 1)
    upto = src <= dst
    same = (src // CHUNK) == (dst // CHUNK)
    sum_full = jnp.where(upto, 1.0, 0.0).astype(bf16)
    sum_chunk = jnp.where(upto & same, 1.0, 0.0).astype(bf16)
    tot_chunk = jnp.where(same, 1.0, 0.0).astype(bf16)
    row = lax.broadcasted_iota(jnp.int32, (nf, blk), 0)
    b_forget = pv_ref[0]
    neg_a = -jnp.exp(pv_ref[1])
    dt_bias = pv_ref[2]
    carry = jnp.zeros((nf, 1), f32)
    for i in range(s // blk):
        x = sm_ref[i * blk:(i + 1) * blk, :].T[0:nf, :]
        xf = x + b_forget
        log_f = jnp.minimum(xf, 0.0) - _softplus_neg_abs(xf)
        beta = jax.nn.sigmoid(x)
        xg = x + dt_bias
        g = neg_a * (jnp.maximum(xg, 0.0) + _softplus_neg_abs(xg))
        val = jnp.where(row < 4, log_f, jnp.where(row < 8, beta, g))
        parts = _split3(val)
        cum_full = sum(_dot(p, sum_full) for p in parts) + carry
        cum_chunk = sum(_dot(p, sum_chunk) for p in parts)
        total = sum(_dot(p, tot_chunk) for p in parts)
        carry = cum_full[:, blk - 1:blk]
        out = jnp.where(row < 4, cum_full,
                        jnp.where(row < 8, val, jnp.where(row < 12, cum_chunk, total)))
        pt_ref[:, i * blk:(i + 1) * blk] = out
        padded = jnp.concatenate([out, jnp.zeros((SMALL_W - nf, blk), f32)], axis=0)
        p_ref[i * blk:(i + 1) * blk, :] = padded.T


def _prefix(sm, pv):
    bsz, s, _ = sm.shape
    return pl.pallas_call(
        _prefix_kernel,
        grid=(bsz,),
        in_specs=[pl.BlockSpec((None, s, SMALL_W), lambda b: (b, 0, 0)), _const_spec(pv.shape)],
        out_specs=[pl.BlockSpec((None, s, SMALL_W), lambda b: (b, 0, 0)),
                   pl.BlockSpec((None, 16, s), lambda b: (b, 0, 0))],
        out_shape=[jax.ShapeDtypeStruct((bsz, s, SMALL_W), f32),
                   jax.ShapeDtypeStruct((bsz, 16, s), f32)],
        compiler_params=_params(("parallel",)),
        name="gate_prefix",
    )(sm, pv)


def _head_masks(shape, axis):
    lane = lax.broadcasted_iota(jnp.int32, shape, axis)
    return [(lane >= hd * ATT_HEAD) & (lane < (hd + 1) * ATT_HEAD) for hd in range(HEADS)]


def _by_head(masks, cols):
    out = cols[HEADS - 1]
    for hd in range(HEADS - 2, -1, -1):
        out = jnp.where(masks[hd], cols[hd], out)
    return out


def _fox_kernel(q_ref, kt_ref, v_ref, p_ref, pt_ref, o_ref,
                vm_ref, s_ref, cq_ref, m_ref, l_ref, acc_ref):
    tq, tk, lw = FOX_TQ, FOX_TK, LANES
    hs = range(HEADS)
    groups = range(tk // lw)
    i = pl.program_id(1)
    q = q_ref[...]
    qmask = _head_masks((tq, ATT_W), 1)
    zero = jnp.zeros((), bf16)
    qm = [jnp.where(qmask[hd], q, zero) for hd in hs]

    @pl.when(i == 0)
    def _():
        vmask = _head_masks(v_ref.shape, 1)
        v = v_ref[...]
        for hd in hs:
            vm_ref[hd] = jnp.where(vmask[hd], v, zero)

    delta = (lax.broadcasted_iota(jnp.int32, (tq, lw), 1)
             - lax.broadcasted_iota(jnp.int32, (tq, lw), 0))
    n_full = (i * tq) // tk

    def score_block(j, _, masked=False):
        off = pl.multiple_of(j * tk, tk)
        kt = kt_ref[:, pl.ds(off, tk)]
        for hd in hs:
            s = _dot(qm[hd], kt)
            cq = cq_ref[hd]
            ck = pt_ref[hd:hd + 1, pl.ds(off, tk)] * LOG2E
            mx = m_ref[hd]
            for g in groups:
                sg = s[:, g * lw:(g + 1) * lw] + cq - ck[:, g * lw:(g + 1) * lw]
                if masked:
                    sg = jnp.where(delta <= i * tq - off - g * lw, sg, NEG_BIG)
                s_ref[hd, j * len(groups) + g] = sg
                mx = jnp.maximum(mx, sg)
            m_ref[hd] = mx
        return 0

    m_ref[...] = jnp.full(m_ref.shape, NEG_BIG, f32)
    for hd in hs:
        cq_ref[hd] = jnp.broadcast_to(p_ref[:, hd:hd + 1], (tq, lw)) * LOG2E
    lax.fori_loop(0, n_full, score_block, 0)
    score_block(n_full, 0, masked=True)
    for hd in hs:
        m_ref[hd] = jnp.broadcast_to(jnp.max(m_ref[hd], axis=-1, keepdims=True), (tq, lw))

    def sum_block(j, _):
        off = pl.multiple_of(j * tk, tk)
        pv = None
        for hd in hs:
            mx = m_ref[hd]
            parts = [jnp.exp2(s_ref[hd, j * len(groups) + g] - mx) for g in groups]
            l = l_ref[hd]
            for g in groups:
                l = l + parts[g]
            l_ref[hd] = l
            p = jnp.concatenate(parts, axis=1)
            c = _dot(p.astype(bf16), vm_ref[hd, pl.ds(off, tk), :])
            pv = c if pv is None else pv + c
        acc_ref[...] += pv
        return 0

    l_ref[...] = jnp.zeros(l_ref.shape, f32)
    acc_ref[...] = jnp.zeros(acc_ref.shape, f32)
    lax.fori_loop(0, n_full + 1, sum_block, 0)
    l = [jnp.sum(l_ref[hd], axis=-1, keepdims=True) for hd in hs]
    o_ref[...] = (acc_ref[...] / _by_head(qmask, l)).astype(bf16)


def _fox(q, kt, v, p, pt):
    bsz, s, _ = q.shape
    tq = FOX_TQ
    return pl.pallas_call(
        _fox_kernel,
        grid=(bsz, s // tq),
        in_specs=[pl.BlockSpec((None, tq, ATT_W), lambda b, i: (b, i, 0)),
                  pl.BlockSpec((None, ATT_W, s), lambda b, i: (b, 0, 0)),
                  pl.BlockSpec((None, s, ATT_W), lambda b, i: (b, 0, 0)),
                  pl.BlockSpec((None, tq, SMALL_W), lambda b, i: (b, i, 0)),
                  pl.BlockSpec((None, 16, s), lambda b, i: (b, 0, 0))],
        out_specs=pl.BlockSpec((None, tq, ATT_W), lambda b, i: (b, i, 0)),
        out_shape=jax.ShapeDtypeStruct((bsz, s, ATT_W), bf16),
        scratch_shapes=[pltpu.VMEM((HEADS, s, ATT_W), bf16),
                        pltpu.VMEM((HEADS, s // LANES, tq, LANES), f32),
                        pltpu.VMEM((HEADS, tq, LANES), f32),
                        pltpu.VMEM((HEADS, tq, LANES), f32),
                        pltpu.VMEM((HEADS, tq, LANES), f32),
                        pltpu.VMEM((tq, ATT_W), f32)],
        compiler_params=_params(("parallel", "arbitrary")),
        name="forgetting_attention",
    )(q, kt, v, p, pt)


def _sb_kernel(q_ref, kt_ref, v_ref, o_ref):
    tq, tk = ATT_TQ, ATT_TK
    i = pl.program_id(1)
    q = q_ref[...]
    qmask = _head_masks((tq, ATT_W), 1)
    vmask = _head_masks((tk, ATT_W), 1)
    zero = jnp.zeros((), bf16)
    qm = [jnp.where(qmask[hd], q, zero) for hd in range(HEADS)]
    rows = lax.broadcasted_iota(jnp.int32, (tq, tk), 0)
    cols = lax.broadcasted_iota(jnp.int32, (tq, tk), 1)
    strict = cols < rows
    kr = lax.broadcasted_iota(jnp.int32, (tk, tk), 0)
    kc = lax.broadcasted_iota(jnp.int32, (tk, tk), 1)
    later = jnp.where(kr > kc, 1.0, 0.0).astype(bf16)

    def block(j, rem, acc, diag):
        off = pl.multiple_of(j * tk, tk)
        kt = kt_ref[:, pl.ds(off, tk)]
        v = v_ref[pl.ds(off, tk), :]
        hs = range(HEADS)
        z = [_dot(qm[hd], kt) for hd in hs]
        log_beta = [jnp.minimum(z[hd], 0.0) - jnp.log2(1.0 + jnp.exp2(-jnp.abs(z[hd])))
                    for hd in hs]
        log_rem = [log_beta[hd] - z[hd] for hd in hs]
        if diag:
            log_rem = [jnp.where(strict, log_rem[hd], 0.0) for hd in hs]
        parts = [_split2(log_rem[hd]) for hd in hs]
        after = [_dot(parts[hd][0], later) + _dot(parts[hd][1], later) + rem[hd] for hd in hs]
        w = [jnp.exp2(log_beta[hd] + after[hd]) for hd in hs]
        if diag:
            w = [jnp.where(strict, w[hd], 0.0) for hd in hs]
        rem_new = [rem[hd] + jnp.sum(log_rem[hd], axis=-1, keepdims=True) for hd in hs]
        for hd in hs:
            acc = acc + _dot(w[hd].astype(bf16), jnp.where(vmask[hd], v, zero))
        return tuple(rem_new), acc

    def first_blocks(n_blocks):
        rem, acc = block(i, tuple(jnp.zeros((tq, 1), f32) for _ in range(HEADS)),
                         jnp.zeros((tq, ATT_W), f32), diag=True)
        if n_blocks == 2:
            rem, acc = block(i - 1, rem, acc, diag=False)
        return rem, acc, i - n_blocks

    rem, acc, j_next = lax.cond(i > 0, lambda: first_blocks(2), lambda: first_blocks(1))

    def live(rem):
        top = rem[0]
        for hd in range(1, HEADS):
            top = jnp.maximum(top, rem[hd])
        return jnp.max(top) > EXP2_UNDERFLOW

    def cond(state):
        return jnp.logical_and(state[0] >= 0, state[1])

    def body(state):
        j, _, rem, acc = state
        rem, acc = block(j, rem, acc, diag=False)
        return j - 1, live(rem), rem, acc

    _, _, _, acc = lax.while_loop(cond, body, (j_next, live(rem), rem, acc))
    o_ref[...] = acc.astype(bf16)


def _sb(q, kt, v):
    bsz, s, _ = q.shape
    tq = ATT_TQ
    return pl.pallas_call(
        _sb_kernel,
        grid=(bsz, s // tq),
        in_specs=[pl.BlockSpec((None, tq, ATT_W), lambda b, i: (b, i, 0)),
                  pl.BlockSpec((None, ATT_W, s), lambda b, i: (b, 0, 0)),
                  pl.BlockSpec((None, s, ATT_W), lambda b, i: (b, 0, 0))],
        out_specs=pl.BlockSpec((None, tq, ATT_W), lambda b, i: (b, i, 0)),
        out_shape=jax.ShapeDtypeStruct((bsz, s, ATT_W), bf16),
        compiler_params=_params(("parallel", "parallel")),
        name="stick_breaking_attention",
    )(q, kt, v)


def _gdn_kernel(q_ref, k_ref, v_ref, p_ref, pt_ref, ng_ref, o_ref,
                qe_ref, ol_ref, g_ref, b_ref, dec_ref, state_ref):
    ts = q_ref.shape[1]
    sb = GDN_SB
    hs = range(HEADS)
    lanes = [slice(hd * GDN_HEAD, (hd + 1) * GDN_HEAD) for hd in hs]
    ri = lax.broadcasted_iota(jnp.int32, (sb, sb), 0)
    ci = lax.broadcasted_iota(jnp.int32, (sb, sb), 1)
    same = (ri // CHUNK) == (ci // CHUNK)
    incl = same & (ri >= ci)
    strict = same & (ri > ci)
    cast = lambda t: t.astype(bf16)

    @pl.when(pl.program_id(1) == 0)
    def _():
        state_ref[...] = jnp.zeros(state_ref.shape, f32)

    def local(i, _):
        units = [(blk, h) for blk in range(GDN_LOCAL_BLOCKS) for h in hs]
        us = range(len(units))
        bi = [i * GDN_LOCAL_BLOCKS + blk for blk, _ in units]
        rows = [pl.ds(pl.multiple_of(bi[u] * sb, sb), sb) for u in us]
        hd = [h for _, h in units]
        k = [k_ref[hd[u], rows[u], :] for u in us]
        beta = [p_ref[rows[u], 4 + hd[u]:5 + hd[u]] for u in us]
        gc = [p_ref[rows[u], 8 + hd[u]:9 + hd[u]] for u in us]
        gl = [p_ref[rows[u], 12 + hd[u]:13 + hd[u]] for u in us]
        decay = [jnp.where(incl, jnp.exp(jnp.where(
            incl, gc[u] - pt_ref[8 + hd[u]:9 + hd[u], rows[u]], 0.0)), 0.0) for u in us]
        kb = [k[u] * beta[u] for u in us]
        kbf = [cast(k[u]) for u in us]
        a = [jnp.where(strict, _dot_nt(cast(kb[u]), kbf[u]) * decay[u], 0.0) for u in us]
        ab = [cast(a[u]) for u in us]
        pw = [_dot(ab[u], ab[u]) for u in us]
        n = [-a[u] for u in us]
        for level in range(5):
            pb = [cast(pw[u]) for u in us]
            n = [n[u] + pw[u] + _dot(cast(n[u]), pb[u]) for u in us]
            if level < 4:
                pw = [_dot(pb[u], pb[u]) for u in us]
        e_gc = [jnp.exp(gc[u]) for u in us]
        rhs = [jnp.concatenate([v_ref[hd[u], rows[u], :] * beta[u], kb[u] * e_gc[u]], axis=1)
               for u in us]
        sol = [rhs[u] + _dot(cast(n[u]), cast(rhs[u])) for u in us]
        solb = [cast(sol[u]) for u in us]
        q = [q_ref[hd[u], rows[u], :] for u in us]
        intra = [cast(jnp.where(incl, _dot_nt(cast(q[u]), kbf[u]) * decay[u], 0.0)) for u in us]
        prod = [_dot(intra[u], solb[u]) for u in us]
        kd = [cast(k[u] * jnp.exp(gl[u] - gc[u])) for u in us]
        for u in us:
            h = hd[u]
            ol_ref[h, rows[u], :] = prod[u][:, :GDN_HEAD]
            qe_ref[h, rows[u], :] = cast(q[u] * e_gc[u] - prod[u][:, GDN_HEAD:])
            glb = jnp.exp(jnp.broadcast_to(gl[u], (sb, GDN_HEAD)))
            for c in range(2):
                crow = slice(c * CHUNK, (c + 1) * CHUNK)
                bg = _dot_tn(kd[u][crow], solb[u][crow])
                chunk = 2 * bi[u] + c
                grow = pl.ds(pl.multiple_of(chunk * GDN_HEAD, GDN_HEAD), GDN_HEAD)
                b_ref[h, grow, :] = bg[:, :GDN_HEAD]
                g_ref[h, grow, :] = cast(bg[:, GDN_HEAD:])
                dec_ref[h, pl.ds(chunk, 1), :] = glb[c * CHUNK:c * CHUNK + 1, :]
        return 0

    lax.fori_loop(0, ts // (sb * GDN_LOCAL_BLOCKS), local, 0)

    def scan(c, _):
        crow = pl.ds(pl.multiple_of(c * CHUNK, CHUNK), CHUNK)
        grow = pl.ds(pl.multiple_of(c * GDN_HEAD, GDN_HEAD), GDN_HEAD)
        state = [state_ref[h] for h in hs]
        sbf = [cast(state[h]) for h in hs]
        gs = [_dot(g_ref[h, grow, :], sbf[h]) for h in hs]
        for h in hs:
            state_ref[h] = state[h] * dec_ref[h, pl.ds(c, 1), :] - gs[h] + b_ref[h, grow, :]
        for h in hs:
            o_ref[h, crow, :] = _dot(qe_ref[h, crow, :], sbf[h]) + ol_ref[h, crow, :]
        return 0

    lax.fori_loop(0, ts // CHUNK, scan, 0)

    def norm(i, _):
        rows = pl.ds(pl.multiple_of(i * GDN_NORM_ROWS, GDN_NORM_ROWS), GDN_NORM_ROWS)
        for h in hs:
            o = o_ref[h, rows, :]
            o_ref[h, rows, :] = (o * lax.rsqrt(jnp.mean(o * o, axis=-1, keepdims=True) + RMS_EPS)
                                     * ng_ref[...])
        return 0

    lax.fori_loop(0, ts // GDN_NORM_ROWS, norm, 0)


def _gdn(q, k, v, p, pt, ng):
    bsz, _, s, _ = q.shape
    ts = min(GDN_TS, s)
    n_chunks = ts // CHUNK
    per_head = pl.BlockSpec((None, HEADS, ts, GDN_HEAD), lambda b, i: (b, 0, i, 0))
    return pl.pallas_call(
        _gdn_kernel,
        grid=(bsz, s // ts),
        in_specs=[per_head, per_head, per_head,
                  pl.BlockSpec((None, ts, SMALL_W), lambda b, i: (b, i, 0)),
                  pl.BlockSpec((None, 16, ts), lambda b, i: (b, 0, i)), _const_spec(ng.shape)],
        out_specs=per_head,
        out_shape=jax.ShapeDtypeStruct((bsz, HEADS, s, GDN_HEAD), f32),
        scratch_shapes=[pltpu.VMEM((HEADS, ts, GDN_HEAD), bf16), pltpu.VMEM((HEADS, ts, GDN_HEAD), f32),
                        pltpu.VMEM((HEADS, n_chunks * GDN_HEAD, GDN_HEAD), bf16),
                        pltpu.VMEM((HEADS, n_chunks * GDN_HEAD, GDN_HEAD), f32),
                        pltpu.VMEM((HEADS, n_chunks, GDN_HEAD), f32),
                        pltpu.VMEM((HEADS, GDN_HEAD, GDN_HEAD), f32)],
        compiler_params=_params(("parallel", "arbitrary")),
        name="gated_delta_rule",
    )(q, k, v, p, pt, ng)


def _merge_kernel(x_ref, mod_ref, oa_ref, ob_ref, on_ref, wgate_ref, wgg_ref,
                  wba_ref, wbb_ref, wbc_ref, wo_ref, lng_ref, lnb_ref, o_ref, h_ref):
    x = x_ref[...]
    d = x.shape[1]
    h_ref[...] = (x * (1.0 + mod_ref[1:2, :]) + mod_ref[0:1, :]).astype(bf16)
    gg = _dot(h_ref[...], wgg_ref[...])
    on = jnp.concatenate([on_ref[hd] for hd in range(HEADS)], axis=1)
    oc = (on * (gg * jax.nn.sigmoid(gg))).astype(bf16)
    branches = ((oa_ref[...], wba_ref), (ob_ref[...], wbb_ref), (oc, wbc_ref))
    merged = None
    for n, (o, wb_ref) in enumerate(branches):
        gate = jax.nn.sigmoid(_dot(h_ref[...], wgate_ref[:, n * d:(n + 1) * d]))
        term = gate * _dot(o, wb_ref[...])
        merged = term if merged is None else merged + term
    y = _dot(merged.astype(bf16), wo_ref[...])
    r = DEEPNORM_ALPHA * x + (1.0 + mod_ref[2:3, :]) * y
    o_ref[...] = _layer_norm(r, lng_ref[...], lnb_ref[...])


def _merge(x, mod, oa, ob, on, wgate, wgg, wba, wbb, wbc, wo, lng, lnb):
    bsz, s, d = x.shape
    tm = MERGE_TM
    row = lambda w: pl.BlockSpec((None, tm, w), lambda b, i: (b, i, 0))
    return pl.pallas_call(
        _merge_kernel,
        grid=(bsz, s // tm),
        in_specs=[row(d), pl.BlockSpec((None, 3, d), lambda b, i: (b, 0, 0)),
                  row(ATT_W), row(ATT_W),
                  pl.BlockSpec((None, HEADS, tm, GDN_HEAD), lambda b, i: (b, 0, i, 0)),
                  _const_spec(wgate.shape), _const_spec(wgg.shape), _const_spec(wba.shape),
                  _const_spec(wbb.shape), _const_spec(wbc.shape), _const_spec(wo.shape),
                  _const_spec((1, d)), _const_spec((1, d))],
        out_specs=row(d),
        out_shape=jax.ShapeDtypeStruct(x.shape, f32),
        scratch_shapes=[pltpu.VMEM((tm, d), bf16)],
        compiler_params=_params(("parallel", "parallel")),
        name="mixer_merge",
    )(x, mod, oa, ob, on, wgate, wgg, wba, wbb, wbc, wo, lng, lnb)


def _mixer_weights(w_in, b_forget, a_log, dt_bias):
    d = w_in.shape[0]
    sizes = (ATT_W, ATT_W, ATT_W, HEADS, ATT_W, ATT_W, ATT_W, 3 * GDN_W, HEADS, HEADS, GDN_W,
             3 * D_MODEL)
    offs = [0]
    for n in sizes:
        offs.append(offs[-1] + n)
    piece = lambda n: w_in[:, offs[n]:offs[n + 1]]
    wa = jnp.concatenate([piece(0), piece(1), piece(2)], axis=1).astype(bf16)
    wb = jnp.concatenate([piece(4), piece(5), piece(6)], axis=1).astype(bf16)
    wc = piece(7).astype(bf16)
    ws = jnp.concatenate([piece(3), piece(8), piece(9), piece(9),
                          jnp.zeros((d, SMALL_W - 4 * HEADS), w_in.dtype)], axis=1).astype(bf16)
    wgg = piece(10).astype(bf16)
    wgate = piece(11).astype(bf16)
    pad = lambda vec, at: jnp.zeros((SMALL_FEATS,), f32).at[at:at + HEADS].set(vec.astype(f32))
    pv = jnp.stack([pad(b_forget, 0),
                    pad(a_log, 2 * HEADS) + pad(a_log, 3 * HEADS),
                    pad(dt_bias, 2 * HEADS) + pad(dt_bias, 3 * HEADS)])
    pv = jnp.broadcast_to(pv[:, :, None], (3, SMALL_FEATS, PREFIX_BLK))
    return wa, wb, wc, ws, wgg, wgate, pv


def kernel(x, c, w_ada, b_ada, ln_g, ln_b, ffn_w_up, ffn_w_down, w_in, b_forget, conv_w,
           gdn_a_log, gdn_dt_bias, gdn_norm_g, w_branch, w_o):
    bsz = x.shape[0]
    mod = _ada(c, w_ada, b_ada).reshape(DEPTH, bsz, N_SUB, 3, D_MODEL)
    wup = ffn_w_up.astype(bf16)
    wdown = ffn_w_down.astype(bf16)
    for l in range(DEPTH):
        lng = lambda n: ln_g[l, n].reshape(1, D_MODEL)
        lnb = lambda n: ln_b[l, n].reshape(1, D_MODEL)
        x = _ffn(x, mod[l, :, 0], wup[l, 0], wdown[l, 0], lng(0), lnb(0))

        wa, wb, wc, ws, wgg, wgate, pv = _mixer_weights(w_in[l], b_forget[l], gdn_a_log[l],
                                                        gdn_dt_bias[l])
        m = mod[l, :, 1]
        fq, fkt, fv, sq, skt, sv, gq, gk, gv, sm = _proj(x, m, wa, wb, wc, ws, conv_w[l])
        p, pt = _prefix(sm, pv)
        oa = _fox(fq, fkt, fv, p, pt)
        ob = _sb(sq, skt, sv)
        on = _gdn(gq, gk, gv, p, pt, gdn_norm_g[l].reshape(1, GDN_HEAD))
        wbr = w_branch[l].astype(bf16)
        x = _merge(x, m, oa, ob, on, wgate, wgg, wbr[:ATT_W], wbr[ATT_W:2 * ATT_W],
                   wbr[2 * ATT_W:], w_o[l].astype(bf16), lng(1), lnb(1))

        x = _ffn(x, mod[l, :, 2], wup[l, 1], wdown[l, 1], lng(2), lnb(2))
    return x
```

```python
import functools

import jax
import jax.numpy as jnp
from jax import lax
from jax.experimental import pallas as pl
from jax.experimental.pallas import tpu as pltpu

f32 = jnp.float32
bf16 = jnp.bfloat16

D_MODEL = 1024
DEPTH = 2
N_SUB = 3
FFN_HIDDEN = 2816
HEADS = 4
ATT_W = 256
ATT_HEAD = 64
GDN_W = 512
GDN_HEAD = 128
CHUNK = 64
CONV_WIDTH = 4
SMALL_W = 128
SMALL_FEATS = 16
DEEPNORM_ALPHA = (2.0 * DEPTH) ** 0.25
LN_EPS = 1e-5
RMS_EPS = 1e-6
LOG2E = 1.4426950408889634

FFN_TM = 512
FFN_HC = 256
PROJ_TM = 512
MERGE_TM = 512
LANES = 128
ATT_TQ = 256
ATT_TK = 256
FOX_TQ = 512
FOX_TK = 512
PREFIX_BLK = 256
GDN_SB = 128
GDN_TS = 1024
GDN_LOCAL_BLOCKS = 4
GDN_SCAN_CHUNKS = 4
GDN_NORM_ROWS = 256
CONV_HALO = 8
NEG_BIG = -1e30
EXP2_UNDERFLOW = -151.0
VMEM_LIMIT = 56 * 1024 * 1024


def _dot(a, b):
    return jnp.dot(a, b, preferred_element_type=f32)


def _dot_nt(a, b):
    return lax.dot_general(a, b, (((1,), (1,)), ((), ())), preferred_element_type=f32)


def _dot_tn(a, b):
    return lax.dot_general(a, b, (((0,), (0,)), ((), ())), preferred_element_type=f32)


def _split2(x):
    hi = x.astype(bf16)
    lo = (x - hi.astype(f32)).astype(bf16)
    return hi, lo


def _split3(x):
    hi = x.astype(bf16)
    r = x - hi.astype(f32)
    mid = r.astype(bf16)
    lo = (r - mid.astype(f32)).astype(bf16)
    return hi, mid, lo


def _dot3(a, b):
    ah, al = _split2(a)
    bh, bl = _split2(b)
    return _dot(ah, bh) + (_dot(ah, bl) + _dot(al, bh))


def _layer_norm(r, g, b):
    mu = jnp.mean(r, axis=-1, keepdims=True)
    d = r - mu
    var = jnp.mean(d * d, axis=-1, keepdims=True)
    return d * lax.rsqrt(var + LN_EPS) * g + b


def _softplus_neg_abs(x):
    return jnp.log1p(jnp.exp(-jnp.abs(x)))


def _const_spec(shape):
    nd = len(shape)
    return pl.BlockSpec(shape, lambda *_: (0,) * nd, pipeline_mode=pl.Buffered(1))


def _params(sem):
    return pltpu.CompilerParams(dimension_semantics=sem, vmem_limit_bytes=VMEM_LIMIT)


def _ada_kernel(c_ref, w_ref, b_ref, o_ref):
    c = c_ref[...]
    sc = c * jax.nn.sigmoid(c)
    o_ref[...] = _dot3(sc, w_ref[...]) + b_ref[...]


def _ada(c, w_ada, b_ada):
    depth, d, n = w_ada.shape
    bsz = c.shape[0]
    tn = 1024
    return pl.pallas_call(
        _ada_kernel,
        grid=(depth, n // tn),
        in_specs=[pl.BlockSpec((bsz, d), lambda l, j: (0, 0)),
                  pl.BlockSpec((None, d, tn), lambda l, j: (l, 0, j)),
                  pl.BlockSpec((None, 1, tn), lambda l, j: (l, 0, j))],
        out_specs=pl.BlockSpec((None, bsz, tn), lambda l, j: (l, 0, j)),
        out_shape=jax.ShapeDtypeStruct((depth, bsz, n), f32),
        compiler_params=_params(("parallel", "parallel")),
        name="adaln",
    )(c, w_ada, b_ada.reshape(depth, 1, n))


def _ffn_kernel(x_ref, mod_ref, wup_ref, wd_ref, lng_ref, lnb_ref, o_ref, h_ref, acc_ref):
    x = x_ref[...]
    h_ref[...] = (x * (1.0 + mod_ref[1:2, :]) + mod_ref[0:1, :]).astype(bf16)
    for j in range(FFN_HIDDEN // FFN_HC):
        lo = j * FFN_HC
        h = h_ref[...]
        g = _dot(h, wup_ref[:, lo:lo + FFN_HC])
        u = _dot(h, wup_ref[:, FFN_HIDDEN + lo:FFN_HIDDEN + lo + FFN_HC])
        a = (g * jax.nn.sigmoid(g) * u).astype(bf16)
        y = _dot(a, wd_ref[lo:lo + FFN_HC, :])
        if j == 0:
            acc_ref[...] = y
        else:
            acc_ref[...] += y
    r = DEEPNORM_ALPHA * x + (0.5 * (1.0 + mod_ref[2:3, :])) * acc_ref[...]
    o_ref[...] = _layer_norm(r, lng_ref[...], lnb_ref[...])


def _ffn(x, mod, wup, wd, lng, lnb):
    bsz, s, d = x.shape
    tm = FFN_TM
    return pl.pallas_call(
        _ffn_kernel,
        grid=(bsz, s // tm),
        in_specs=[pl.BlockSpec((None, tm, d), lambda b, i: (b, i, 0)),
                  pl.BlockSpec((None, 3, d), lambda b, i: (b, 0, 0)),
                  _const_spec(wup.shape), _const_spec(wd.shape),
                  _const_spec((1, d)), _const_spec((1, d))],
        out_specs=pl.BlockSpec((None, tm, d), lambda b, i: (b, i, 0)),
        out_shape=jax.ShapeDtypeStruct(x.shape, f32),
        scratch_shapes=[pltpu.VMEM((tm, d), bf16), pltpu.VMEM((tm, d), f32)],
        compiler_params=_params(("parallel", "parallel")),
        name="ffn",
    )(x, mod, wup, wd, lng, lnb)


def _proj_kernel(x_ref, mod_ref, wa_ref, wb_ref, wc_ref, ws_ref, cw_ref,
                 fq_ref, fkt_ref, fv_ref, sq_ref, skt_ref, sv_ref,
                 gq_ref, gk_ref, gv_ref, sm_ref, h_ref, cbuf_ref):
    tm = x_ref.shape[0]
    h_ref[...] = (x_ref[...] * (1.0 + mod_ref[1:2, :]) + mod_ref[0:1, :]).astype(bf16)

    def attn_group(w_ref, q_ref, kt_ref, v_ref, q_scale):
        r = _dot(h_ref[...], w_ref[...])
        q_ref[...] = (r[:, :ATT_W] * q_scale).astype(bf16)
        kt_ref[...] = r[:, ATT_W:2 * ATT_W].T.astype(bf16)
        v_ref[...] = r[:, 2 * ATT_W:].astype(bf16)

    @pl.when(pl.program_id(1) == 0)
    def _():
        cbuf_ref[0:CONV_HALO, :] = jnp.zeros((CONV_HALO, cbuf_ref.shape[1]), f32)

    def conv_act(part):
        cols = slice(part * GDN_W, (part + 1) * GDN_W)
        cbuf_ref[CONV_HALO:, cols] = _dot(h_ref[...], wc_ref[:, cols])
        conv = cbuf_ref[CONV_HALO:, cols] * cw_ref[CONV_WIDTH - 1:CONV_WIDTH, cols]
        for tap in range(CONV_WIDTH - 1):
            back = CONV_WIDTH - 1 - tap
            conv = conv + (cbuf_ref[CONV_HALO - back:CONV_HALO - back + tm, cols]
                           * cw_ref[tap:tap + 1, cols])
        cbuf_ref[0:CONV_HALO, cols] = cbuf_ref[tm:tm + CONV_HALO, cols]
        return conv * jax.nn.sigmoid(conv)

    def l2norm_heads(act, o_ref, scale):
        for hd in range(HEADS):
            lanes = slice(hd * GDN_HEAD, (hd + 1) * GDN_HEAD)
            t = act[:, lanes]
            o_ref[hd] = t * lax.rsqrt(jnp.sum(t * t, -1, keepdims=True) + RMS_EPS) * scale

    l2norm_heads(conv_act(0), gq_ref, GDN_HEAD ** -0.5)
    attn_group(wa_ref, fq_ref, fkt_ref, fv_ref, LOG2E * ATT_HEAD ** -0.5)
    l2norm_heads(conv_act(1), gk_ref, 1.0)
    attn_group(wb_ref, sq_ref, skt_ref, sv_ref, LOG2E * ATT_HEAD ** -0.5)
    act_v = conv_act(2)
    for hd in range(HEADS):
        gv_ref[hd] = act_v[:, hd * GDN_HEAD:(hd + 1) * GDN_HEAD]
    sm_ref[...] = _dot(h_ref[...], ws_ref[...])


def _proj(x, mod, wa, wb, wc, ws, cw):
    bsz, s, d = x.shape
    tm = PROJ_TM
    row = lambda w: pl.BlockSpec((None, tm, w), lambda b, i: (b, i, 0))
    col = pl.BlockSpec((None, ATT_W, tm), lambda b, i: (b, 0, i))
    per_head = pl.BlockSpec((None, HEADS, tm, GDN_HEAD), lambda b, i: (b, 0, i, 0))
    sd = jax.ShapeDtypeStruct
    return pl.pallas_call(
        _proj_kernel,
        grid=(bsz, s // tm),
        in_specs=[row(d), pl.BlockSpec((None, 3, d), lambda b, i: (b, 0, 0)),
                  _const_spec(wa.shape), _const_spec(wb.shape), _const_spec(wc.shape),
                  _const_spec(ws.shape), _const_spec(cw.shape)],
        out_specs=[row(ATT_W), col, row(ATT_W), row(ATT_W), col, row(ATT_W),
                   per_head, per_head, per_head, row(SMALL_W)],
        out_shape=[sd((bsz, s, ATT_W), bf16), sd((bsz, ATT_W, s), bf16), sd((bsz, s, ATT_W), bf16),
                   sd((bsz, s, ATT_W), bf16), sd((bsz, ATT_W, s), bf16), sd((bsz, s, ATT_W), bf16),
                   sd((bsz, HEADS, s, GDN_HEAD), f32), sd((bsz, HEADS, s, GDN_HEAD), f32),
                   sd((bsz, HEADS, s, GDN_HEAD), f32), sd((bsz, s, SMALL_W), f32)],
        scratch_shapes=[pltpu.VMEM((tm, d), bf16), pltpu.VMEM((tm + CONV_HALO, 3 * GDN_W), f32)],
        compiler_params=_params(("parallel", "arbitrary")),
        name="mixer_in_proj",
    )(x, mod, wa, wb, wc, ws, cw)


def _prefix_kernel(sm_ref, pv_ref, p_ref, pt_ref):
    s = sm_ref.shape[0]
    blk, nf = PREFIX_BLK, SMALL_FEATS
    src = lax.broadcasted_iota(jnp.int32, (blk, blk), 0)
    dst = lax.broadcasted_iota(jnp.int32, (blk, blk), 1)
    upto = src <= dst
    same = (src // CHUNK) == (dst // CHUNK)
    sum_full = jnp.where(upto, 1.0, 0.0).astype(bf16)
    sum_chunk = jnp.where(upto & same, 1.0, 0.0).astype(bf16)
    tot_chunk = jnp.where(same, 1.0, 0.0).astype(bf16)
    row = lax.broadcasted_iota(jnp.int32, (nf, blk), 0)
    b_forget = pv_ref[0]
    neg_a = -jnp.exp(pv_ref[1])
    dt_bias = pv_ref[2]
    carry = jnp.zeros((nf, 1), f32)
    for i in range(s // blk):
        x = sm_ref[i * blk:(i + 1) * blk, :].T[0:nf, :]
        xf = x + b_forget
        log_f = jnp.minimum(xf, 0.0) - _softplus_neg_abs(xf)
        beta = jax.nn.sigmoid(x)
        xg = x + dt_bias
        g = neg_a * (jnp.maximum(xg, 0.0) + _softplus_neg_abs(xg))
        val = jnp.where(row < 4, log_f, jnp.where(row < 8, beta, g))
        parts = _split3(val)
        cum_full = sum(_dot(p, sum_full) for p in parts) + carry
        cum_chunk = sum(_dot(p, sum_chunk) for p in parts)
        total = sum(_dot(p, tot_chunk) for p in parts)
        carry = cum_full[:, blk - 1:blk]
        out = jnp.where(row < 4, cum_full,
                        jnp.where(row < 8, val, jnp.where(row < 12, cum_chunk, total)))
        pt_ref[:, i * blk:(i + 1) * blk] = out
        padded = jnp.concatenate([out, jnp.zeros((SMALL_W - nf, blk), f32)], axis=0)
        p_ref[i * blk:(i + 1) * blk, :] = padded.T


def _prefix(sm, pv):
    bsz, s, _ = sm.shape
    return pl.pallas_call(
        _prefix_kernel,
        grid=(bsz,),
        in_specs=[pl.BlockSpec((None, s, SMALL_W), lambda b: (b, 0, 0)), _const_spec(pv.shape)],
        out_specs=[pl.BlockSpec((None, s, SMALL_W), lambda b: (b, 0, 0)),
                   pl.BlockSpec((None, 16, s), lambda b: (b, 0, 0))],
        out_shape=[jax.ShapeDtypeStruct((bsz, s, SMALL_W), f32),
                   jax.ShapeDtypeStruct((bsz, 16, s), f32)],
        compiler_params=_params(("parallel",)),
        name="gate_prefix",
    )(sm, pv)


def _head_masks(shape, axis):
    lane = lax.broadcasted_iota(jnp.int32, shape, axis)
    return [(lane >= hd * ATT_HEAD) & (lane < (hd + 1) * ATT_HEAD) for hd in range(HEADS)]


def _by_head(masks, cols):
    out = cols[HEADS - 1]
    for hd in range(HEADS - 2, -1, -1):
        out = jnp.where(masks[hd], cols[hd], out)
    return out


def _fox_kernel(q_ref, kt_ref, v_ref, p_ref, pt_ref, o_ref,
                vm_ref, s_ref, cq_ref, m_ref, l_ref, acc_ref):
    tq, tk, lw = FOX_TQ, FOX_TK, LANES
    hs = range(HEADS)
    groups = range(tk // lw)
    i = pl.program_id(1)
    q = q_ref[...]
    qmask = _head_masks((tq, ATT_W), 1)
    zero = jnp.zeros((), bf16)
    qm = [jnp.where(qmask[hd], q, zero) for hd in hs]

    @pl.when(i == 0)
    def _():
        vmask = _head_masks(v_ref.shape, 1)
        v = v_ref[...]
        for hd in hs:
            vm_ref[hd] = jnp.where(vmask[hd], v, zero)

    delta = (lax.broadcasted_iota(jnp.int32, (tq, lw), 1)
             - lax.broadcasted_iota(jnp.int32, (tq, lw), 0))
    n_full = (i * tq) // tk

    def score_block(j, _, masked=False):
        off = pl.multiple_of(j * tk, tk)
        kt = kt_ref[:, pl.ds(off, tk)]
        for hd in hs:
            s = _dot(qm[hd], kt)
            cq = cq_ref[hd]
            ck = pt_ref[hd:hd + 1, pl.ds(off, tk)] * LOG2E
            mx = m_ref[hd]
            for g in groups:
                sg = s[:, g * lw:(g + 1) * lw] + cq - ck[:, g * lw:(g + 1) * lw]
                if masked:
                    sg = jnp.where(delta <= i * tq - off - g * lw, sg, NEG_BIG)
                s_ref[hd, j * len(groups) + g] = sg
                mx = jnp.maximum(mx, sg)
            m_ref[hd] = mx
        return 0

    m_ref[...] = jnp.full(m_ref.shape, NEG_BIG, f32)
    for hd in hs:
        cq_ref[hd] = jnp.broadcast_to(p_ref[:, hd:hd + 1], (tq, lw)) * LOG2E
    lax.fori_loop(0, n_full, score_block, 0)
    score_block(n_full, 0, masked=True)
    for hd in hs:
        m_ref[hd] = jnp.broadcast_to(jnp.max(m_ref[hd], axis=-1, keepdims=True), (tq, lw))

    def sum_block(j, _):
        off = pl.multiple_of(j * tk, tk)
        pv = None
        for hd in hs:
            mx = m_ref[hd]
            parts = [jnp.exp2(s_ref[hd, j * len(groups) + g] - mx) for g in groups]
            l = l_ref[hd]
            for g in groups:
                l = l + parts[g]
            l_ref[hd] = l
            p = jnp.concatenate(parts, axis=1)
            c = _dot(p.astype(bf16), vm_ref[hd, pl.ds(off, tk), :])
            pv = c if pv is None else pv + c
        acc_ref[...] += pv
        return 0

    l_ref[...] = jnp.zeros(l_ref.shape, f32)
    acc_ref[...] = jnp.zeros(acc_ref.shape, f32)
    lax.fori_loop(0, n_full + 1, sum_block, 0)
    l = [jnp.sum(l_ref[hd], axis=-1, keepdims=True) for hd in hs]
    o_ref[...] = (acc_ref[...] / _by_head(qmask, l)).astype(bf16)


def _fox(q, kt, v, p, pt):
    bsz, s, _ = q.shape
    tq = FOX_TQ
    return pl.pallas_call(
        _fox_kernel,
        grid=(bsz, s // tq),
        in_specs=[pl.BlockSpec((None, tq, ATT_W), lambda b, i: (b, i, 0)),
                  pl.BlockSpec((None, ATT_W, s), lambda b, i: (b, 0, 0)),
                  pl.BlockSpec((None, s, ATT_W), lambda b, i: (b, 0, 0)),
                  pl.BlockSpec((None, tq, SMALL_W), lambda b, i: (b, i, 0)),
                  pl.BlockSpec((None, 16, s), lambda b, i: (b, 0, 0))],
        out_specs=pl.BlockSpec((None, tq, ATT_W), lambda b, i: (b, i, 0)),
        out_shape=jax.ShapeDtypeStruct((bsz, s, ATT_W), bf16),
        scratch_shapes=[pltpu.VMEM((HEADS, s, ATT_W), bf16),
                        pltpu.VMEM((HEADS, s // LANES, tq, LANES), f32),
                        pltpu.VMEM((HEADS, tq, LANES), f32),
                        pltpu.VMEM((HEADS, tq, LANES), f32),
                        pltpu.VMEM((HEADS, tq, LANES), f32),
                        pltpu.VMEM((tq, ATT_W), f32)],
        compiler_params=_params(("parallel", "arbitrary")),
        name="forgetting_attention",
    )(q, kt, v, p, pt)


def _sb_kernel(q_ref, kt_ref, v_ref, o_ref):
    tq, tk = ATT_TQ, ATT_TK
    i = pl.program_id(1)
    q = q_ref[...]
    qmask = _head_masks((tq, ATT_W), 1)
    vmask = _head_masks((tk, ATT_W), 1)
    zero = jnp.zeros((), bf16)
    qm = [jnp.where(qmask[hd], q, zero) for hd in range(HEADS)]
    rows = lax.broadcasted_iota(jnp.int32, (tq, tk), 0)
    cols = lax.broadcasted_iota(jnp.int32, (tq, tk), 1)
    strict = cols < rows
    kr = lax.broadcasted_iota(jnp.int32, (tk, tk), 0)
    kc = lax.broadcasted_iota(jnp.int32, (tk, tk), 1)
    later = jnp.where(kr > kc, 1.0, 0.0).astype(bf16)

    def block(j, rem, acc, diag):
        off = pl.multiple_of(j * tk, tk)
        kt = kt_ref[:, pl.ds(off, tk)]
        v = v_ref[pl.ds(off, tk), :]
        hs = range(HEADS)
        z = [_dot(qm[hd], kt) for hd in hs]
        log_beta = [jnp.minimum(z[hd], 0.0) - jnp.log2(1.0 + jnp.exp2(-jnp.abs(z[hd])))
                    for hd in hs]
        log_rem = [log_beta[hd] - z[hd] for hd in hs]
        if diag:
            log_rem = [jnp.where(strict, log_rem[hd], 0.0) for hd in hs]
        parts = [_split2(log_rem[hd]) for hd in hs]
        after = [_dot(parts[hd][0], later) + _dot(parts[hd][1], later) + rem[hd] for hd in hs]
        w = [jnp.exp2(log_beta[hd] + after[hd]) for hd in hs]
        if diag:
            w = [jnp.where(strict, w[hd], 0.0) for hd in hs]
        rem_new = [rem[hd] + jnp.sum(log_rem[hd], axis=-1, keepdims=True) for hd in hs]
        for hd in hs:
            acc = acc + _dot(w[hd].astype(bf16), jnp.where(vmask[hd], v, zero))
        return tuple(rem_new), acc

    def first_blocks(n_blocks):
        rem, acc = block(i, tuple(jnp.zeros((tq, 1), f32) for _ in range(HEADS)),
                         jnp.zeros((tq, ATT_W), f32), diag=True)
        if n_blocks == 2:
            rem, acc = block(i - 1, rem, acc, diag=False)
        return rem, acc, i - n_blocks

    rem, acc, j_next = lax.cond(i > 0, lambda: first_blocks(2), lambda: first_blocks(1))

    def live(rem):
        top = rem[0]
        for hd in range(1, HEADS):
            top = jnp.maximum(top, rem[hd])
        return jnp.max(top) > EXP2_UNDERFLOW

    def cond(state):
        return jnp.logical_and(state[0] >= 0, state[1])

    def body(state):
        j, _, rem, acc = state
        rem, acc = block(j, rem, acc, diag=False)
        return j - 1, live(rem), rem, acc

    _, _, _, acc = lax.while_loop(cond, body, (j_next, live(rem), rem, acc))
    o_ref[...] = acc.astype(bf16)


def _sb(q, kt, v):
    bsz, s, _ = q.shape
    tq = ATT_TQ
    return pl.pallas_call(
        _sb_kernel,
        grid=(bsz, s // tq),
        in_specs=[pl.BlockSpec((None, tq, ATT_W), lambda b, i: (b, i, 0)),
                  pl.BlockSpec((None, ATT_W, s), lambda b, i: (b, 0, 0)),
                  pl.BlockSpec((None, s, ATT_W), lambda b, i: (b, 0, 0))],
        out_specs=pl.BlockSpec((None, tq, ATT_W), lambda b, i: (b, i, 0)),
        out_shape=jax.ShapeDtypeStruct((bsz, s, ATT_W), bf16),
        compiler_params=_params(("parallel", "parallel")),
        name="stick_breaking_attention",
    )(q, kt, v)


def _gdn_kernel(q_ref, k_ref, v_ref, p_ref, pt_ref, ng_ref, o_ref,
                qe_ref, ol_ref, g_ref, b_ref, dec_ref, state_ref):
    ts = q_ref.shape[1]
    sb = GDN_SB
    hs = range(HEADS)
    lanes = [slice(hd * GDN_HEAD, (hd + 1) * GDN_HEAD) for hd in hs]
    ri = lax.broadcasted_iota(jnp.int32, (sb, sb), 0)
    ci = lax.broadcasted_iota(jnp.int32, (sb, sb), 1)
    same = (ri // CHUNK) == (ci // CHUNK)
    incl = same & (ri >= ci)
    strict = same & (ri > ci)
    cast = lambda t: t.astype(bf16)

    @pl.when(pl.program_id(1) == 0)
    def _():
        state_ref[...] = jnp.zeros(state_ref.shape, f32)

    def local(i, _):
        units = [(blk, h) for blk in range(GDN_LOCAL_BLOCKS) for h in hs]
        us = range(len(units))
        bi = [i * GDN_LOCAL_BLOCKS + blk for blk, _ in units]
        rows = [pl.ds(pl.multiple_of(bi[u] * sb, sb), sb) for u in us]
        hd = [h for _, h in units]
        k = [k_ref[hd[u], rows[u], :] for u in us]
        beta = [p_ref[rows[u], 4 + hd[u]:5 + hd[u]] for u in us]
        gc = [p_ref[rows[u], 8 + hd[u]:9 + hd[u]] for u in us]
        gl = [p_ref[rows[u], 12 + hd[u]:13 + hd[u]] for u in us]
        decay = [jnp.where(incl, jnp.exp(jnp.where(
            incl, gc[u] - pt_ref[8 + hd[u]:9 + hd[u], rows[u]], 0.0)), 0.0) for u in us]
        kb = [k[u] * beta[u] for u in us]
        kbf = [cast(k[u]) for u in us]
        a = [jnp.where(strict, _dot_nt(cast(kb[u]), kbf[u]) * decay[u], 0.0) for u in us]
        ab = [cast(a[u]) for u in us]
        pw = [_dot(ab[u], ab[u]) for u in us]
        n = [-a[u] for u in us]
        for level in range(5):
            pb = [cast(pw[u]) for u in us]
            n = [n[u] + pw[u] + _dot(cast(n[u]), pb[u]) for u in us]
            if level < 4:
                pw = [_dot(pb[u], pb[u]) for u in us]
        e_gc = [jnp.exp(gc[u]) for u in us]
        rhs = [jnp.concatenate([v_ref[hd[u], rows[u], :] * beta[u], kb[u] * e_gc[u]], axis=1)
               for u in us]
        sol = [rhs[u] + _dot(cast(n[u]), cast(rhs[u])) for u in us]
        solb = [cast(sol[u]) for u in us]
        q = [q_ref[hd[u], rows[u], :] for u in us]
        intra = [cast(jnp.where(incl, _dot_nt(cast(q[u]), kbf[u]) * decay[u], 0.0)) for u in us]
        prod = [_dot(intra[u], solb[u]) for u in us]
        kd = [cast(k[u] * jnp.exp(gl[u] - gc[u])) for u in us]
        for u in us:
            h = hd[u]
            ol_ref[h, rows[u], :] = prod[u][:, :GDN_HEAD]
            qe_ref[h, rows[u], :] = cast(q[u] * e_gc[u] - prod[u][:, GDN_HEAD:])
            glb = jnp.exp(jnp.broadcast_to(gl[u], (sb, GDN_HEAD)))
            for c in range(2):
                crow = slice(c * CHUNK, (c + 1) * CHUNK)
                bg = _dot_tn(kd[u][crow], solb[u][crow])
                chunk = 2 * bi[u] + c
                grow = pl.ds(pl.multiple_of(chunk * GDN_HEAD, GDN_HEAD), GDN_HEAD)
                b_ref[h, grow, :] = bg[:, :GDN_HEAD]
                g_ref[h, grow, :] = cast(bg[:, GDN_HEAD:])
                dec_ref[h, pl.ds(chunk, 1), :] = glb[c * CHUNK:c * CHUNK + 1, :]
        return 0

    lax.fori_loop(0, ts // (sb * GDN_LOCAL_BLOCKS), local, 0)

    def scan(i, _):
        state = [state_ref[h] for h in hs]
        for step in range(GDN_SCAN_CHUNKS):
            c = i * GDN_SCAN_CHUNKS + step
            crow = pl.ds(pl.multiple_of(c * CHUNK, CHUNK), CHUNK)
            grow = pl.ds(pl.multiple_of(c * GDN_HEAD, GDN_HEAD), GDN_HEAD)
            sbf = [cast(state[h]) for h in hs]
            gs = [_dot(g_ref[h, grow, :], sbf[h]) for h in hs]
            state = [state[h] * dec_ref[h, pl.ds(c, 1), :] - gs[h] + b_ref[h, grow, :] for h in hs]
            for h in hs:
                o_ref[h, crow, :] = _dot(qe_ref[h, crow, :], sbf[h]) + ol_ref[h, crow, :]
        for h in hs:
            state_ref[h] = state[h]
        return 0

    lax.fori_loop(0, ts // (CHUNK * GDN_SCAN_CHUNKS), scan, 0)

    def norm(i, _):
        rows = pl.ds(pl.multiple_of(i * GDN_NORM_ROWS, GDN_NORM_ROWS), GDN_NORM_ROWS)
        for h in hs:
            o = o_ref[h, rows, :]
            o_ref[h, rows, :] = (o * lax.rsqrt(jnp.mean(o * o, axis=-1, keepdims=True) + RMS_EPS)
                                     * ng_ref[...])
        return 0

    lax.fori_loop(0, ts // GDN_NORM_ROWS, norm, 0)


def _gdn(q, k, v, p, pt, ng):
    bsz, _, s, _ = q.shape
    ts = min(GDN_TS, s)
    n_chunks = ts // CHUNK
    per_head = pl.BlockSpec((None, HEADS, ts, GDN_HEAD), lambda b, i: (b, 0, i, 0))
    return pl.pallas_call(
        _gdn_kernel,
        grid=(bsz, s // ts),
        in_specs=[per_head, per_head, per_head,
                  pl.BlockSpec((None, ts, SMALL_W), lambda b, i: (b, i, 0)),
                  pl.BlockSpec((None, 16, ts), lambda b, i: (b, 0, i)), _const_spec(ng.shape)],
        out_specs=per_head,
        out_shape=jax.ShapeDtypeStruct((bsz, HEADS, s, GDN_HEAD), f32),
        scratch_shapes=[pltpu.VMEM((HEADS, ts, GDN_HEAD), bf16), pltpu.VMEM((HEADS, ts, GDN_HEAD), f32),
                        pltpu.VMEM((HEADS, n_chunks * GDN_HEAD, GDN_HEAD), bf16),
                        pltpu.VMEM((HEADS, n_chunks * GDN_HEAD, GDN_HEAD), f32),
                        pltpu.VMEM((HEADS, n_chunks, GDN_HEAD), f32),
                        pltpu.VMEM((HEADS, GDN_HEAD, GDN_HEAD), f32)],
        compiler_params=_params(("parallel", "arbitrary")),
        name="gated_delta_rule",
    )(q, k, v, p, pt, ng)


def _merge_kernel(x_ref, mod_ref, oa_ref, ob_ref, on_ref, wgate_ref, wgg_ref,
                  wba_ref, wbb_ref, wbc_ref, wo_ref, lng_ref, lnb_ref, o_ref, h_ref):
    x = x_ref[...]
    d = x.shape[1]
    h_ref[...] = (x * (1.0 + mod_ref[1:2, :]) + mod_ref[0:1, :]).astype(bf16)
    gg = _dot(h_ref[...], wgg_ref[...])
    on = jnp.concatenate([on_ref[hd] for hd in range(HEADS)], axis=1)
    oc = (on * (gg * jax.nn.sigmoid(gg))).astype(bf16)
    branches = ((oa_ref[...], wba_ref), (ob_ref[...], wbb_ref), (oc, wbc_ref))
    merged = None
    for n, (o, wb_ref) in enumerate(branches):
        gate = jax.nn.sigmoid(_dot(h_ref[...], wgate_ref[:, n * d:(n + 1) * d]))
        term = gate * _dot(o, wb_ref[...])
        merged = term if merged is None else merged + term
    y = _dot(merged.astype(bf16), wo_ref[...])
    r = DEEPNORM_ALPHA * x + (1.0 + mod_ref[2:3, :]) * y
    o_ref[...] = _layer_norm(r, lng_ref[...], lnb_ref[...])


def _merge(x, mod, oa, ob, on, wgate, wgg, wba, wbb, wbc, wo, lng, lnb):
    bsz, s, d = x.shape
    tm = MERGE_TM
    row = lambda w: pl.BlockSpec((None, tm, w), lambda b, i: (b, i, 0))
    return pl.pallas_call(
        _merge_kernel,
        grid=(bsz, s // tm),
        in_specs=[row(d), pl.BlockSpec((None, 3, d), lambda b, i: (b, 0, 0)),
                  row(ATT_W), row(ATT_W),
                  pl.BlockSpec((None, HEADS, tm, GDN_HEAD), lambda b, i: (b, 0, i, 0)),
                  _const_spec(wgate.shape), _const_spec(wgg.shape), _const_spec(wba.shape),
                  _const_spec(wbb.shape), _const_spec(wbc.shape), _const_spec(wo.shape),
                  _const_spec((1, d)), _const_spec((1, d))],
        out_specs=row(d),
        out_shape=jax.ShapeDtypeStruct(x.shape, f32),
        scratch_shapes=[pltpu.VMEM((tm, d), bf16)],
        compiler_params=_params(("parallel", "parallel")),
        name="mixer_merge",
    )(x, mod, oa, ob, on, wgate, wgg, wba, wbb, wbc, wo, lng, lnb)


def _mixer_weights(w_in, b_forget, a_log, dt_bias):
    d = w_in.shape[0]
    sizes = (ATT_W, ATT_W, ATT_W, HEADS, ATT_W, ATT_W, ATT_W, 3 * GDN_W, HEADS, HEADS, GDN_W,
             3 * D_MODEL)
    offs = [0]
    for n in sizes:
        offs.append(offs[-1] + n)
    piece = lambda n: w_in[:, offs[n]:offs[n + 1]]
    wa = jnp.concatenate([piece(0), piece(1), piece(2)], axis=1).astype(bf16)
    wb = jnp.concatenate([piece(4), piece(5), piece(6)], axis=1).astype(bf16)
    wc = piece(7).astype(bf16)
    ws = jnp.concatenate([piece(3), piece(8), piece(9), piece(9),
                          jnp.zeros((d, SMALL_W - 4 * HEADS), w_in.dtype)], axis=1).astype(bf16)
    wgg = piece(10).astype(bf16)
    wgate = piece(11).astype(bf16)
    pad = lambda vec, at: jnp.zeros((SMALL_FEATS,), f32).at[at:at + HEADS].set(vec.astype(f32))
    pv = jnp.stack([pad(b_forget, 0),
                    pad(a_log, 2 * HEADS) + pad(a_log, 3 * HEADS),
                    pad(dt_bias, 2 * HEADS) + pad(dt_bias, 3 * HEADS)])
    pv = jnp.broadcast_to(pv[:, :, None], (3, SMALL_FEATS, PREFIX_BLK))
    return wa, wb, wc, ws, wgg, wgate, pv


def kernel(x, c, w_ada, b_ada, ln_g, ln_b, ffn_w_up, ffn_w_down, w_in, b_forget, conv_w,
           gdn_a_log, gdn_dt_bias, gdn_norm_g, w_branch, w_o):
    bsz = x.shape[0]
    mod = _ada(c, w_ada, b_ada).reshape(DEPTH, bsz, N_SUB, 3, D_MODEL)
    wup = ffn_w_up.astype(bf16)
    wdown = ffn_w_down.astype(bf16)
    for l in range(DEPTH):
        lng = lambda n: ln_g[l, n].reshape(1, D_MODEL)
        lnb = lambda n: ln_b[l, n].reshape(1, D_MODEL)
        x = _ffn(x, mod[l, :, 0], wup[l, 0], wdown[l, 0], lng(0), lnb(0))

        wa, wb, wc, ws, wgg, wgate, pv = _mixer_weights(w_in[l], b_forget[l], gdn_a_log[l],
                                                        gdn_dt_bias[l])
        m = mod[l, :, 1]
        fq, fkt, fv, sq, skt, sv, gq, gk, gv, sm = _proj(x, m, wa, wb, wc, ws, conv_w[l])
        p, pt = _prefix(sm, pv)
        oa = _fox(fq, fkt, fv, p, pt)
        ob = _sb(sq, skt, sv)
        on = _gdn(gq, gk, gv, p, pt, gdn_norm_g[l].reshape(1, GDN_HEAD))
        wbr = w_branch[l].astype(bf16)
        x = _merge(x, m, oa, ob, on, wgate, wgg, wbr[:ATT_W], wbr[ATT_W:2 * ATT_W],
                   wbr[2 * ATT_W:], w_o[l].astype(bf16), lng(1), lnb(1))

        x = _ffn(x, mod[l, :, 2], wup[l, 1], wdown[l, 1], lng(2), lnb(2))
    return x
```

```python
import functools

import jax
import jax.numpy as jnp
from jax import lax
from jax.experimental import pallas as pl
from jax.experimental.pallas import tpu as pltpu

f32 = jnp.float32
bf16 = jnp.bfloat16

D_MODEL = 1024
DEPTH = 2
N_SUB = 3
FFN_HIDDEN = 2816
HEADS = 4
ATT_W = 256
ATT_HEAD = 64
GDN_W = 512
GDN_HEAD = 128
CHUNK = 64
CONV_WIDTH = 4
SMALL_W = 128
SMALL_FEATS = 16
DEEPNORM_ALPHA = (2.0 * DEPTH) ** 0.25
LN_EPS = 1e-5
RMS_EPS = 1e-6
LOG2E = 1.4426950408889634

FFN_TM = 512
FFN_HC = 256
PROJ_TM = 512
MERGE_TM = 512
LANES = 128
ATT_TQ = 256
ATT_TK = 256
FOX_TQ = 512
FOX_TK = 512
PREFIX_BLK = 256
GDN_SB = 128
GDN_TS = 1024
GDN_LOCAL_BLOCKS = 4
GDN_SCAN_CHUNKS = 4
GDN_NORM_ROWS = 256
CONV_HALO = 8
NEG_BIG = -1e30
EXP2_UNDERFLOW = -151.0
VMEM_LIMIT = 56 * 1024 * 1024


def _dot(a, b):
    return jnp.dot(a, b, preferred_element_type=f32)


def _dot_nt(a, b):
    return lax.dot_general(a, b, (((1,), (1,)), ((), ())), preferred_element_type=f32)


def _dot_tn(a, b):
    return lax.dot_general(a, b, (((0,), (0,)), ((), ())), preferred_element_type=f32)


def _split2(x):
    hi = x.astype(bf16)
    lo = (x - hi.astype(f32)).astype(bf16)
    return hi, lo


def _split3(x):
    hi = x.astype(bf16)
    r = x - hi.astype(f32)
    mid = r.astype(bf16)
    lo = (r - mid.astype(f32)).astype(bf16)
    return hi, mid, lo


def _dot3(a, b):
    ah, al = _split2(a)
    bh, bl = _split2(b)
    return _dot(ah, bh) + (_dot(ah, bl) + _dot(al, bh))


def _layer_norm(r, g, b):
    mu = jnp.mean(r, axis=-1, keepdims=True)
    d = r - mu
    var = jnp.mean(d * d, axis=-1, keepdims=True)
    return d * lax.rsqrt(var + LN_EPS) * g + b


def _softplus_neg_abs(x):
    return jnp.log1p(jnp.exp(-jnp.abs(x)))


def _const_spec(shape):
    nd = len(shape)
    return pl.BlockSpec(shape, lambda *_: (0,) * nd, pipeline_mode=pl.Buffered(1))


def _params(sem):
    return pltpu.CompilerParams(dimension_semantics=sem, vmem_limit_bytes=VMEM_LIMIT)


def _ada_kernel(c_ref, w_ref, b_ref, o_ref):
    c = c_ref[...]
    sc = c * jax.nn.sigmoid(c)
    o_ref[...] = _dot3(sc, w_ref[...]) + b_ref[...]


def _ada(c, w_ada, b_ada):
    depth, d, n = w_ada.shape
    bsz = c.shape[0]
    tn = 1024
    return pl.pallas_call(
        _ada_kernel,
        grid=(depth, n // tn),
        in_specs=[pl.BlockSpec((bsz, d), lambda l, j: (0, 0)),
                  pl.BlockSpec((None, d, tn), lambda l, j: (l, 0, j)),
                  pl.BlockSpec((None, 1, tn), lambda l, j: (l, 0, j))],
        out_specs=pl.BlockSpec((None, bsz, tn), lambda l, j: (l, 0, j)),
        out_shape=jax.ShapeDtypeStruct((depth, bsz, n), f32),
        compiler_params=_params(("parallel", "parallel")),
        name="adaln",
    )(c, w_ada, b_ada.reshape(depth, 1, n))


def _ffn_kernel(x_ref, mod_ref, wup_ref, wd_ref, lng_ref, lnb_ref, o_ref, h_ref, acc_ref):
    x = x_ref[...]
    h_ref[...] = (x * (1.0 + mod_ref[1:2, :]) + mod_ref[0:1, :]).astype(bf16)
    for j in range(FFN_HIDDEN // FFN_HC):
        lo = j * FFN_HC
        h = h_ref[...]
        g = _dot(h, wup_ref[:, lo:lo + FFN_HC])
        u = _dot(h, wup_ref[:, FFN_HIDDEN + lo:FFN_HIDDEN + lo + FFN_HC])
        a = (g * jax.nn.sigmoid(g) * u).astype(bf16)
        y = _dot(a, wd_ref[lo:lo + FFN_HC, :])
        if j == 0:
            acc_ref[...] = y
        else:
            acc_ref[...] += y
    r = DEEPNORM_ALPHA * x + (0.5 * (1.0 + mod_ref[2:3, :])) * acc_ref[...]
    o_ref[...] = _layer_norm(r, lng_ref[...], lnb_ref[...])


def _ffn(x, mod, wup, wd, lng, lnb):
    bsz, s, d = x.shape
    tm = FFN_TM
    return pl.pallas_call(
        _ffn_kernel,
        grid=(bsz, s // tm),
        in_specs=[pl.BlockSpec((None, tm, d), lambda b, i: (b, i, 0)),
                  pl.BlockSpec((None, 3, d), lambda b, i: (b, 0, 0)),
                  _const_spec(wup.shape), _const_spec(wd.shape),
                  _const_spec((1, d)), _const_spec((1, d))],
        out_specs=pl.BlockSpec((None, tm, d), lambda b, i: (b, i, 0)),
        out_shape=jax.ShapeDtypeStruct(x.shape, f32),
        scratch_shapes=[pltpu.VMEM((tm, d), bf16), pltpu.VMEM((tm, d), f32)],
        compiler_params=_params(("parallel", "parallel")),
        name="ffn",
    )(x, mod, wup, wd, lng, lnb)


def _proj_kernel(x_ref, mod_ref, wa_ref, wb_ref, wc_ref, ws_ref, cw_ref,
                 fq_ref, fkt_ref, fv_ref, sq_ref, skt_ref, sv_ref,
                 gq_ref, gk_ref, gv_ref, sm_ref, h_ref, cbuf_ref):
    tm = x_ref.shape[0]
    h_ref[...] = (x_ref[...] * (1.0 + mod_ref[1:2, :]) + mod_ref[0:1, :]).astype(bf16)

    def attn_group(w_ref, q_ref, kt_ref, v_ref, q_scale):
        r = _dot(h_ref[...], w_ref[...])
        q_ref[...] = (r[:, :ATT_W] * q_scale).astype(bf16)
        kt_ref[...] = r[:, ATT_W:2 * ATT_W].T.astype(bf16)
        v_ref[...] = r[:, 2 * ATT_W:].astype(bf16)

    @pl.when(pl.program_id(1) == 0)
    def _():
        cbuf_ref[0:CONV_HALO, :] = jnp.zeros((CONV_HALO, cbuf_ref.shape[1]), f32)

    def conv_act(part):
        cols = slice(part * GDN_W, (part + 1) * GDN_W)
        cbuf_ref[CONV_HALO:, cols] = _dot(h_ref[...], wc_ref[:, cols])
        conv = cbuf_ref[CONV_HALO:, cols] * cw_ref[CONV_WIDTH - 1:CONV_WIDTH, cols]
        for tap in range(CONV_WIDTH - 1):
            back = CONV_WIDTH - 1 - tap
            conv = conv + (cbuf_ref[CONV_HALO - back:CONV_HALO - back + tm, cols]
                           * cw_ref[tap:tap + 1, cols])
        cbuf_ref[0:CONV_HALO, cols] = cbuf_ref[tm:tm + CONV_HALO, cols]
        return conv * jax.nn.sigmoid(conv)

    def l2norm_heads(act, o_ref, scale):
        for hd in range(HEADS):
            lanes = slice(hd * GDN_HEAD, (hd + 1) * GDN_HEAD)
            t = act[:, lanes]
            o_ref[hd] = t * lax.rsqrt(jnp.sum(t * t, -1, keepdims=True) + RMS_EPS) * scale

    l2norm_heads(conv_act(0), gq_ref, GDN_HEAD ** -0.5)
    attn_group(wa_ref, fq_ref, fkt_ref, fv_ref, LOG2E * ATT_HEAD ** -0.5)
    l2norm_heads(conv_act(1), gk_ref, 1.0)
    attn_group(wb_ref, sq_ref, skt_ref, sv_ref, LOG2E * ATT_HEAD ** -0.5)
    act_v = conv_act(2)
    for hd in range(HEADS):
        gv_ref[hd] = act_v[:, hd * GDN_HEAD:(hd + 1) * GDN_HEAD]
    sm_ref[...] = _dot(h_ref[...], ws_ref[...])


def _proj(x, mod, wa, wb, wc, ws, cw):
    bsz, s, d = x.shape
    tm = PROJ_TM
    row = lambda w: pl.BlockSpec((None, tm, w), lambda b, i: (b, i, 0))
    col = pl.BlockSpec((None, ATT_W, tm), lambda b, i: (b, 0, i))
    per_head = pl.BlockSpec((None, HEADS, tm, GDN_HEAD), lambda b, i: (b, 0, i, 0))
    sd = jax.ShapeDtypeStruct
    return pl.pallas_call(
        _proj_kernel,
        grid=(bsz, s // tm),
        in_specs=[row(d), pl.BlockSpec((None, 3, d), lambda b, i: (b, 0, 0)),
                  _const_spec(wa.shape), _const_spec(wb.shape), _const_spec(wc.shape),
                  _const_spec(ws.shape), _const_spec(cw.shape)],
        out_specs=[row(ATT_W), col, row(ATT_W), row(ATT_W), col, row(ATT_W),
                   per_head, per_head, per_head, row(SMALL_W)],
        out_shape=[sd((bsz, s, ATT_W), bf16), sd((bsz, ATT_W, s), bf16), sd((bsz, s, ATT_W), bf16),
                   sd((bsz, s, ATT_W), bf16), sd((bsz, ATT_W, s), bf16), sd((bsz, s, ATT_W), bf16),
                   sd((bsz, HEADS, s, GDN_HEAD), f32), sd((bsz, HEADS, s, GDN_HEAD), f32),
                   sd((bsz, HEADS, s, GDN_HEAD), f32), sd((bsz, s, SMALL_W), f32)],
        scratch_shapes=[pltpu.VMEM((tm, d), bf16), pltpu.VMEM((tm + CONV_HALO, 3 * GDN_W), f32)],
        compiler_params=_params(("parallel", "arbitrary")),
        name="mixer_in_proj",
    )(x, mod, wa, wb, wc, ws, cw)


def _prefix_kernel(sm_ref, pv_ref, p_ref, pt_ref):
    s = sm_ref.shape[0]
    blk, nf = PREFIX_BLK, SMALL_FEATS
    src = lax.broadcasted_iota(jnp.int32, (blk, blk), 0)
    dst = lax.broadcasted_iota(jnp.int32, (blk, blk), 1)
    upto = src <= dst
    same = (src // CHUNK) == (dst // CHUNK)
    sum_full = jnp.where(upto, 1.0, 0.0).astype(bf16)
    sum_chunk = jnp.where(upto & same, 1.0, 0.0).astype(bf16)
    tot_chunk = jnp.where(same, 1.0, 0.0).astype(bf16)
    row = lax.broadcasted_iota(jnp.int32, (nf, blk), 0)
    b_forget = pv_ref[0]
    neg_a = -jnp.exp(pv_ref[1])
    dt_bias = pv_ref[2]
    carry = jnp.zeros((nf, 1), f32)
    for i in range(s // blk):
        x = sm_ref[i * blk:(i + 1) * blk, :].T[0:nf, :]
        xf = x + b_forget
        log_f = jnp.minimum(xf, 0.0) - _softplus_neg_abs(xf)
        beta = jax.nn.sigmoid(x)
        xg = x + dt_bias
        g = neg_a * (jnp.maximum(xg, 0.0) + _softplus_neg_abs(xg))
        val = jnp.where(row < 4, log_f, jnp.where(row < 8, beta, g))
        parts = _split3(val)
        cum_full = sum(_dot(p, sum_full) for p in parts) + carry
        cum_chunk = sum(_dot(p, sum_chunk) for p in parts)
        total = sum(_dot(p, tot_chunk) for p in parts)
        carry = cum_full[:, blk - 1:blk]
        out = jnp.where(row < 4, cum_full,
                        jnp.where(row < 8, val, jnp.where(row < 12, cum_chunk, total)))
        pt_ref[:, i * blk:(i + 1) * blk] = out
        padded = jnp.concatenate([out, jnp.zeros((SMALL_W - nf, blk), f32)], axis=0)
        p_ref[i * blk:(i + 1) * blk, :] = padded.T


def _prefix(sm, pv):
    bsz, s, _ = sm.shape
    return pl.pallas_call(
        _prefix_kernel,
        grid=(bsz,),
        in_specs=[pl.BlockSpec((None, s, SMALL_W), lambda b: (b, 0, 0)), _const_spec(pv.shape)],
        out_specs=[pl.BlockSpec((None, s, SMALL_W), lambda b: (b, 0, 0)),
                   pl.BlockSpec((None, 16, s), lambda b: (b, 0, 0))],
        out_shape=[jax.ShapeDtypeStruct((bsz, s, SMALL_W), f32),
                   jax.ShapeDtypeStruct((bsz, 16, s), f32)],
        compiler_params=_params(("parallel",)),
        name="gate_prefix",
    )(sm, pv)


def _head_masks(shape, axis):
    lane = lax.broadcasted_iota(jnp.int32, shape, axis)
    return [(lane >= hd * ATT_HEAD) & (lane < (hd + 1) * ATT_HEAD) for hd in range(HEADS)]


def _by_head(masks, cols):
    out = cols[HEADS - 1]
    for hd in range(HEADS - 2, -1, -1):
        out = jnp.where(masks[hd], cols[hd], out)
    return out


def _fox_kernel(q_ref, kt_ref, v_ref, p_ref, pt_ref, o_ref,
                vm_ref, s_ref, cq_ref, m_ref, l_ref, acc_ref):
    tq, tk, lw = FOX_TQ, FOX_TK, LANES
    hs = range(HEADS)
    groups = range(tk // lw)
    i = pl.program_id(1)
    q = q_ref[...]
    qmask = _head_masks((tq, ATT_W), 1)
    zero = jnp.zeros((), bf16)
    qm = [jnp.where(qmask[hd], q, zero) for hd in hs]

    @pl.when(i == 0)
    def _():
        vmask = _head_masks(v_ref.shape, 1)
        v = v_ref[...]
        for hd in hs:
            vm_ref[hd] = jnp.where(vmask[hd], v, zero)

    delta = (lax.broadcasted_iota(jnp.int32, (tq, lw), 1)
             - lax.broadcasted_iota(jnp.int32, (tq, lw), 0))
    all_rows = slice(0, tq)
    diag_parts = [(slice(0, tq // 2), tq // 2 // lw), (slice(tq // 2, tq), tk // lw)]

    def score_block(j, rows, n_groups, diag):
        off = pl.multiple_of(j * tk, tk)
        kt = kt_ref[:, pl.ds(off, n_groups * lw)]
        for hd in hs:
            s = _dot(qm[hd][rows], kt)
            cq = cq_ref[hd, rows, :]
            ck = pt_ref[hd:hd + 1, pl.ds(off, n_groups * lw)] * LOG2E
            mx = m_ref[hd, rows, :]
            for g in range(n_groups):
                sg = s[:, g * lw:(g + 1) * lw] + cq - ck[:, g * lw:(g + 1) * lw]
                if diag and (g + 1) * lw > rows.start + 1:
                    sg = jnp.where(delta[rows] <= -g * lw, sg, NEG_BIG)
                s_ref[hd, j * len(groups) + g, rows, :] = sg
                mx = jnp.maximum(mx, sg)
            m_ref[hd, rows, :] = mx
        return 0

    m_ref[...] = jnp.full(m_ref.shape, NEG_BIG, f32)
    for hd in hs:
        cq_ref[hd] = jnp.broadcast_to(p_ref[:, hd:hd + 1], (tq, lw)) * LOG2E
    lax.fori_loop(0, i, lambda j, _: score_block(j, all_rows, len(groups), False), 0)
    for rows, n_groups in diag_parts:
        score_block(i, rows, n_groups, True)
    for hd in hs:
        m_ref[hd] = jnp.broadcast_to(jnp.max(m_ref[hd], axis=-1, keepdims=True), (tq, lw))

    def sum_block(j, rows, n_groups):
        off = pl.multiple_of(j * tk, tk)
        pv = None
        for hd in hs:
            mx = m_ref[hd, rows, :]
            parts = [jnp.exp2(s_ref[hd, j * len(groups) + g, rows, :] - mx) for g in range(n_groups)]
            l = l_ref[hd, rows, :]
            for part in parts:
                l = l + part
            l_ref[hd, rows, :] = l
            p = jnp.concatenate(parts, axis=1)
            c = _dot(p.astype(bf16), vm_ref[hd, pl.ds(off, n_groups * lw), :])
            pv = c if pv is None else pv + c
        acc_ref[rows, :] += pv
        return 0

    l_ref[...] = jnp.zeros(l_ref.shape, f32)
    acc_ref[...] = jnp.zeros(acc_ref.shape, f32)
    lax.fori_loop(0, i, lambda j, _: sum_block(j, all_rows, len(groups)), 0)
    for rows, n_groups in diag_parts:
        sum_block(i, rows, n_groups)
    l = [jnp.sum(l_ref[hd], axis=-1, keepdims=True) for hd in hs]
    o_ref[...] = (acc_ref[...] / _by_head(qmask, l)).astype(bf16)


def _fox(q, kt, v, p, pt):
    bsz, s, _ = q.shape
    tq = FOX_TQ
    assert FOX_TQ == FOX_TK and s % tq == 0
    return pl.pallas_call(
        _fox_kernel,
        grid=(bsz, s // tq),
        in_specs=[pl.BlockSpec((None, tq, ATT_W), lambda b, i: (b, i, 0)),
                  pl.BlockSpec((None, ATT_W, s), lambda b, i: (b, 0, 0)),
                  pl.BlockSpec((None, s, ATT_W), lambda b, i: (b, 0, 0)),
                  pl.BlockSpec((None, tq, SMALL_W), lambda b, i: (b, i, 0)),
                  pl.BlockSpec((None, 16, s), lambda b, i: (b, 0, 0))],
        out_specs=pl.BlockSpec((None, tq, ATT_W), lambda b, i: (b, i, 0)),
        out_shape=jax.ShapeDtypeStruct((bsz, s, ATT_W), bf16),
        scratch_shapes=[pltpu.VMEM((HEADS, s, ATT_W), bf16),
                        pltpu.VMEM((HEADS, s // LANES, tq, LANES), f32),
                        pltpu.VMEM((HEADS, tq, LANES), f32),
                        pltpu.VMEM((HEADS, tq, LANES), f32),
                        pltpu.VMEM((HEADS, tq, LANES), f32),
                        pltpu.VMEM((tq, ATT_W), f32)],
        compiler_params=_params(("parallel", "arbitrary")),
        name="forgetting_attention",
    )(q, kt, v, p, pt)


def _sb_kernel(q_ref, kt_ref, v_ref, o_ref):
    tq, tk = ATT_TQ, ATT_TK
    i = pl.program_id(1)
    q = q_ref[...]
    qmask = _head_masks((tq, ATT_W), 1)
    vmask = _head_masks((tk, ATT_W), 1)
    zero = jnp.zeros((), bf16)
    qm = [jnp.where(qmask[hd], q, zero) for hd in range(HEADS)]
    rows = lax.broadcasted_iota(jnp.int32, (tq, tk), 0)
    cols = lax.broadcasted_iota(jnp.int32, (tq, tk), 1)
    strict = cols < rows
    kr = lax.broadcasted_iota(jnp.int32, (tk, tk), 0)
    kc = lax.broadcasted_iota(jnp.int32, (tk, tk), 1)
    later = jnp.where(kr > kc, 1.0, 0.0).astype(bf16)

    def block(j, rem, acc, diag):
        off = pl.multiple_of(j * tk, tk)
        kt = kt_ref[:, pl.ds(off, tk)]
        v = v_ref[pl.ds(off, tk), :]
        hs = range(HEADS)
        z = [_dot(qm[hd], kt) for hd in hs]
        log_beta = [jnp.minimum(z[hd], 0.0) - jnp.log2(1.0 + jnp.exp2(-jnp.abs(z[hd])))
                    for hd in hs]
        log_rem = [log_beta[hd] - z[hd] for hd in hs]
        if diag:
            log_rem = [jnp.where(strict, log_rem[hd], 0.0) for hd in hs]
        parts = [_split2(log_rem[hd]) for hd in hs]
        after = [_dot(parts[hd][0], later) + _dot(parts[hd][1], later) + rem[hd] for hd in hs]
        w = [jnp.exp2(log_beta[hd] + after[hd]) for hd in hs]
        if diag:
            w = [jnp.where(strict, w[hd], 0.0) for hd in hs]
        rem_new = [rem[hd] + jnp.sum(log_rem[hd], axis=-1, keepdims=True) for hd in hs]
        for hd in hs:
            acc = acc + _dot(w[hd].astype(bf16), jnp.where(vmask[hd], v, zero))
        return tuple(rem_new), acc

    def first_blocks(n_blocks):
        rem, acc = block(i, tuple(jnp.zeros((tq, 1), f32) for _ in range(HEADS)),
                         jnp.zeros((tq, ATT_W), f32), diag=True)
        if n_blocks == 2:
            rem, acc = block(i - 1, rem, acc, diag=False)
        return rem, acc, i - n_blocks

    rem, acc, j_next = lax.cond(i > 0, lambda: first_blocks(2), lambda: first_blocks(1))

    def live(rem):
        top = rem[0]
        for hd in range(1, HEADS):
            top = jnp.maximum(top, rem[hd])
        return jnp.max(top) > EXP2_UNDERFLOW

    def cond(state):
        return jnp.logical_and(state[0] >= 0, state[1])

    def body(state):
        j, _, rem, acc = state
        rem, acc = block(j, rem, acc, diag=False)
        return j - 1, live(rem), rem, acc

    _, _, _, acc = lax.while_loop(cond, body, (j_next, live(rem), rem, acc))
    o_ref[...] = acc.astype(bf16)


def _sb(q, kt, v):
    bsz, s, _ = q.shape
    tq = ATT_TQ
    return pl.pallas_call(
        _sb_kernel,
        grid=(bsz, s // tq),
        in_specs=[pl.BlockSpec((None, tq, ATT_W), lambda b, i: (b, i, 0)),
                  pl.BlockSpec((None, ATT_W, s), lambda b, i: (b, 0, 0)),
                  pl.BlockSpec((None, s, ATT_W), lambda b, i: (b, 0, 0))],
        out_specs=pl.BlockSpec((None, tq, ATT_W), lambda b, i: (b, i, 0)),
        out_shape=jax.ShapeDtypeStruct((bsz, s, ATT_W), bf16),
        compiler_params=_params(("parallel", "parallel")),
        name="stick_breaking_attention",
    )(q, kt, v)


def _gdn_kernel(q_ref, k_ref, v_ref, p_ref, pt_ref, ng_ref, o_ref,
                qe_ref, ol_ref, g_ref, b_ref, dec_ref, state_ref):
    ts = q_ref.shape[1]
    sb = GDN_SB
    hs = range(HEADS)
    lanes = [slice(hd * GDN_HEAD, (hd + 1) * GDN_HEAD) for hd in hs]
    ri = lax.broadcasted_iota(jnp.int32, (sb, sb), 0)
    ci = lax.broadcasted_iota(jnp.int32, (sb, sb), 1)
    same = (ri // CHUNK) == (ci // CHUNK)
    incl = same & (ri >= ci)
    strict = same & (ri > ci)
    cast = lambda t: t.astype(bf16)

    @pl.when(pl.program_id(1) == 0)
    def _():
        state_ref[...] = jnp.zeros(state_ref.shape, f32)

    def local(i, _):
        units = [(blk, h) for blk in range(GDN_LOCAL_BLOCKS) for h in hs]
        us = range(len(units))
        bi = [i * GDN_LOCAL_BLOCKS + blk for blk, _ in units]
        rows = [pl.ds(pl.multiple_of(bi[u] * sb, sb), sb) for u in us]
        hd = [h for _, h in units]
        k = [k_ref[hd[u], rows[u], :] for u in us]
        beta = [p_ref[rows[u], 4 + hd[u]:5 + hd[u]] for u in us]
        gc = [p_ref[rows[u], 8 + hd[u]:9 + hd[u]] for u in us]
        gl = [p_ref[rows[u], 12 + hd[u]:13 + hd[u]] for u in us]
        decay = [jnp.where(incl, jnp.exp(jnp.where(
            incl, gc[u] - pt_ref[8 + hd[u]:9 + hd[u], rows[u]], 0.0)), 0.0) for u in us]
        kb = [k[u] * beta[u] for u in us]
        kbf = [cast(k[u]) for u in us]
        a = [jnp.where(strict, _dot_nt(cast(kb[u]), kbf[u]) * decay[u], 0.0) for u in us]
        ab = [cast(a[u]) for u in us]
        pw = [_dot(ab[u], ab[u]) for u in us]
        n = [-a[u] for u in us]
        for level in range(5):
            pb = [cast(pw[u]) for u in us]
            n = [n[u] + pw[u] + _dot(cast(n[u]), pb[u]) for u in us]
            if level < 4:
                pw = [_dot(pb[u], pb[u]) for u in us]
        e_gc = [jnp.exp(gc[u]) for u in us]
        rhs = [jnp.concatenate([v_ref[hd[u], rows[u], :] * beta[u], kb[u] * e_gc[u]], axis=1)
               for u in us]
        sol = [rhs[u] + _dot(cast(n[u]), cast(rhs[u])) for u in us]
        solb = [cast(sol[u]) for u in us]
        q = [q_ref[hd[u], rows[u], :] for u in us]
        intra = [cast(jnp.where(incl, _dot_nt(cast(q[u]), kbf[u]) * decay[u], 0.0)) for u in us]
        prod = [_dot(intra[u], solb[u]) for u in us]
        kd = [cast(k[u] * jnp.exp(gl[u] - gc[u])) for u in us]
        for u in us:
            h = hd[u]
            ol_ref[h, rows[u], :] = prod[u][:, :GDN_HEAD]
            qe_ref[h, rows[u], :] = cast(q[u] * e_gc[u] - prod[u][:, GDN_HEAD:])
            glb = jnp.exp(jnp.broadcast_to(gl[u], (sb, GDN_HEAD)))
            for c in range(2):
                crow = slice(c * CHUNK, (c + 1) * CHUNK)
                bg = _dot_tn(kd[u][crow], solb[u][crow])
                chunk = 2 * bi[u] + c
                grow = pl.ds(pl.multiple_of(chunk * GDN_HEAD, GDN_HEAD), GDN_HEAD)
                b_ref[h, grow, :] = bg[:, :GDN_HEAD]
                g_ref[h, grow, :] = cast(bg[:, GDN_HEAD:])
                dec_ref[h, pl.ds(chunk, 1), :] = glb[c * CHUNK:c * CHUNK + 1, :]
        return 0

    lax.fori_loop(0, ts // (sb * GDN_LOCAL_BLOCKS), local, 0)

    def scan(i, _):
        state = [state_ref[h] for h in hs]
        for step in range(GDN_SCAN_CHUNKS):
            c = i * GDN_SCAN_CHUNKS + step
            crow = pl.ds(pl.multiple_of(c * CHUNK, CHUNK), CHUNK)
            grow = pl.ds(pl.multiple_of(c * GDN_HEAD, GDN_HEAD), GDN_HEAD)
            sbf = [cast(state[h]) for h in hs]
            gs = [_dot(g_ref[h, grow, :], sbf[h]) for h in hs]
            state = [state[h] * dec_ref[h, pl.ds(c, 1), :] - gs[h] + b_ref[h, grow, :] for h in hs]
            for h in hs:
                o_ref[h, crow, :] = _dot(qe_ref[h, crow, :], sbf[h]) + ol_ref[h, crow, :]
        for h in hs:
            state_ref[h] = state[h]
        return 0

    lax.fori_loop(0, ts // (CHUNK * GDN_SCAN_CHUNKS), scan, 0)

    def norm(i, _):
        rows = pl.ds(pl.multiple_of(i * GDN_NORM_ROWS, GDN_NORM_ROWS), GDN_NORM_ROWS)
        for h in hs:
            o = o_ref[h, rows, :]
            o_ref[h, rows, :] = (o * lax.rsqrt(jnp.mean(o * o, axis=-1, keepdims=True) + RMS_EPS)
                                     * ng_ref[...])
        return 0

    lax.fori_loop(0, ts // GDN_NORM_ROWS, norm, 0)


def _gdn(q, k, v, p, pt, ng):
    bsz, _, s, _ = q.shape
    ts = min(GDN_TS, s)
    n_chunks = ts // CHUNK
    per_head = pl.BlockSpec((None, HEADS, ts, GDN_HEAD), lambda b, i: (b, 0, i, 0))
    return pl.pallas_call(
        _gdn_kernel,
        grid=(bsz, s // ts),
        in_specs=[per_head, per_head, per_head,
                  pl.BlockSpec((None, ts, SMALL_W), lambda b, i: (b, i, 0)),
                  pl.BlockSpec((None, 16, ts), lambda b, i: (b, 0, i)), _const_spec(ng.shape)],
        out_specs=per_head,
        out_shape=jax.ShapeDtypeStruct((bsz, HEADS, s, GDN_HEAD), f32),
        scratch_shapes=[pltpu.VMEM((HEADS, ts, GDN_HEAD), bf16), pltpu.VMEM((HEADS, ts, GDN_HEAD), f32),
                        pltpu.VMEM((HEADS, n_chunks * GDN_HEAD, GDN_HEAD), bf16),
                        pltpu.VMEM((HEADS, n_chunks * GDN_HEAD, GDN_HEAD), f32),
                        pltpu.VMEM((HEADS, n_chunks, GDN_HEAD), f32),
                        pltpu.VMEM((HEADS, GDN_HEAD, GDN_HEAD), f32)],
        compiler_params=_params(("parallel", "arbitrary")),
        name="gated_delta_rule",
    )(q, k, v, p, pt, ng)


def _merge_kernel(x_ref, mod_ref, oa_ref, ob_ref, on_ref, wgate_ref, wgg_ref,
                  wba_ref, wbb_ref, wbc_ref, wo_ref, lng_ref, lnb_ref, o_ref, h_ref):
    x = x_ref[...]
    d = x.shape[1]
    h_ref[...] = (x * (1.0 + mod_ref[1:2, :]) + mod_ref[0:1, :]).astype(bf16)
    gg = _dot(h_ref[...], wgg_ref[...])
    on = jnp.concatenate([on_ref[hd] for hd in range(HEADS)], axis=1)
    oc = (on * (gg * jax.nn.sigmoid(gg))).astype(bf16)
    branches = ((oa_ref[...], wba_ref), (ob_ref[...], wbb_ref), (oc, wbc_ref))
    merged = None
    for n, (o, wb_ref) in enumerate(branches):
        gate = jax.nn.sigmoid(_dot(h_ref[...], wgate_ref[:, n * d:(n + 1) * d]))
        term = gate * _dot(o, wb_ref[...])
        merged = term if merged is None else merged + term
    y = _dot(merged.astype(bf16), wo_ref[...])
    r = DEEPNORM_ALPHA * x + (1.0 + mod_ref[2:3, :]) * y
    o_ref[...] = _layer_norm(r, lng_ref[...], lnb_ref[...])


def _merge(x, mod, oa, ob, on, wgate, wgg, wba, wbb, wbc, wo, lng, lnb):
    bsz, s, d = x.shape
    tm = MERGE_TM
    row = lambda w: pl.BlockSpec((None, tm, w), lambda b, i: (b, i, 0))
    return pl.pallas_call(
        _merge_kernel,
        grid=(bsz, s // tm),
        in_specs=[row(d), pl.BlockSpec((None, 3, d), lambda b, i: (b, 0, 0)),
                  row(ATT_W), row(ATT_W),
                  pl.BlockSpec((None, HEADS, tm, GDN_HEAD), lambda b, i: (b, 0, i, 0)),
                  _const_spec(wgate.shape), _const_spec(wgg.shape), _const_spec(wba.shape),
                  _const_spec(wbb.shape), _const_spec(wbc.shape), _const_spec(wo.shape),
                  _const_spec((1, d)), _const_spec((1, d))],
        out_specs=row(d),
        out_shape=jax.ShapeDtypeStruct(x.shape, f32),
        scratch_shapes=[pltpu.VMEM((tm, d), bf16)],
        compiler_params=_params(("parallel", "parallel")),
        name="mixer_merge",
    )(x, mod, oa, ob, on, wgate, wgg, wba, wbb, wbc, wo, lng, lnb)


def _mixer_weights(w_in, b_forget, a_log, dt_bias):
    d = w_in.shape[0]
    sizes = (ATT_W, ATT_W, ATT_W, HEADS, ATT_W, ATT_W, ATT_W, 3 * GDN_W, HEADS, HEADS, GDN_W,
             3 * D_MODEL)
    offs = [0]
    for n in sizes:
        offs.append(offs[-1] + n)
    piece = lambda n: w_in[:, offs[n]:offs[n + 1]]
    wa = jnp.concatenate([piece(0), piece(1), piece(2)], axis=1).astype(bf16)
    wb = jnp.concatenate([piece(4), piece(5), piece(6)], axis=1).astype(bf16)
    wc = piece(7).astype(bf16)
    ws = jnp.concatenate([piece(3), piece(8), piece(9), piece(9),
                          jnp.zeros((d, SMALL_W - 4 * HEADS), w_in.dtype)], axis=1).astype(bf16)
    wgg = piece(10).astype(bf16)
    wgate = piece(11).astype(bf16)
    pad = lambda vec, at: jnp.zeros((SMALL_FEATS,), f32).at[at:at + HEADS].set(vec.astype(f32))
    pv = jnp.stack([pad(b_forget, 0),
                    pad(a_log, 2 * HEADS) + pad(a_log, 3 * HEADS),
                    pad(dt_bias, 2 * HEADS) + pad(dt_bias, 3 * HEADS)])
    pv = jnp.broadcast_to(pv[:, :, None], (3, SMALL_FEATS, PREFIX_BLK))
    return wa, wb, wc, ws, wgg, wgate, pv


def kernel(x, c, w_ada, b_ada, ln_g, ln_b, ffn_w_up, ffn_w_down, w_in, b_forget, conv_w,
           gdn_a_log, gdn_dt_bias, gdn_norm_g, w_branch, w_o):
    bsz = x.shape[0]
    mod = _ada(c, w_ada, b_ada).reshape(DEPTH, bsz, N_SUB, 3, D_MODEL)
    wup = ffn_w_up.astype(bf16)
    wdown = ffn_w_down.astype(bf16)
    for l in range(DEPTH):
        lng = lambda n: ln_g[l, n].reshape(1, D_MODEL)
        lnb = lambda n: ln_b[l, n].reshape(1, D_MODEL)
        x = _ffn(x, mod[l, :, 0], wup[l, 0], wdown[l, 0], lng(0), lnb(0))

        wa, wb, wc, ws, wgg, wgate, pv = _mixer_weights(w_in[l], b_forget[l], gdn_a_log[l],
                                                        gdn_dt_bias[l])
        m = mod[l, :, 1]
        fq, fkt, fv, sq, skt, sv, gq, gk, gv, sm = _proj(x, m, wa, wb, wc, ws, conv_w[l])
        p, pt = _prefix(sm, pv)
        oa = _fox(fq, fkt, fv, p, pt)
        ob = _sb(sq, skt, sv)
        on = _gdn(gq, gk, gv, p, pt, gdn_norm_g[l].reshape(1, GDN_HEAD))
        wbr = w_branch[l].astype(bf16)
        x = _merge(x, m, oa, ob, on, wgate, wgg, wbr[:ATT_W], wbr[ATT_W:2 * ATT_W],
                   wbr[2 * ATT_W:], w_o[l].astype(bf16), lng(1), lnb(1))

        x = _ffn(x, mod[l, :, 2], wup[l, 1], wdown[l, 1], lng(2), lnb(2))
    return x
```
